```python
import math
import jax
import jax.numpy as jnp
from jax import lax
import numpy as np

D_MODEL = 4096
BATCH = 4
SEQ = 4096
DEPTH = 4

CTX_LEN = 256
GRID_W = 64
N_GROUPS = 4
GROUP_WIDTH = D_MODEL // N_GROUPS
MIX_WIDTH = N_GROUPS * GROUP_WIDTH
ADA_CHUNKS = 6
EPS = 1e-6
ROPE_BASE = 10000.0

GDN_HEADS = 8
GDN_DIM = GROUP_WIDTH // GDN_HEADS
GDN_CONV = 5
GDN_CHUNK = 64

RET_HEADS = 8
RET_DV = GROUP_WIDTH // RET_HEADS
RET_DK = RET_DV // 2
RET_CHUNK = 64
RET_MIN_EXP = 5.0

POOL_WINDOWS = (2, 4, 8, 16)
POOL_GROUP = GROUP_WIDTH // len(POOL_WINDOWS)

DIFF_HEADS = 8
DIFF_DV = GROUP_WIDTH // DIFF_HEADS
DIFF_DK = DIFF_DV // 2
DIFF_QBLOCK = 128

N_EXPERTS = 16
EXPERT_FF = D_MODEL // 16
EC_CAPACITY = 2

IN_SIZES = (
    GROUP_WIDTH, GROUP_WIDTH, GROUP_WIDTH, GROUP_WIDTH,
    2 * GDN_HEADS, 2 * GDN_HEADS,
    RET_HEADS * RET_DK, RET_HEADS * RET_DK, GROUP_WIDTH, GROUP_WIDTH,
    GROUP_WIDTH,
    2 * DIFF_HEADS * DIFF_DK, 2 * DIFF_HEADS * DIFF_DK, GROUP_WIDTH,
)
IN_WIDTH = sum(IN_SIZES)
SPLIT_POINTS = tuple(int(s) for s in np.cumsum(IN_SIZES)[:-1])

kernel_name = 'hybrid_headgroup_dit_ec_moe'


def rms_norm(x, g):
    xf = x.astype(jnp.float32)
    y = xf * lax.rsqrt(jnp.mean(xf * xf, axis=-1, keepdims=True) + EPS)
    return (y * g.astype(jnp.float32)).astype(x.dtype)


def l2_norm(x):
    xf = x.astype(jnp.float32)
    return xf * lax.rsqrt(jnp.sum(xf * xf, axis=-1, keepdims=True) + EPS)


def modulate(x, shift, scale):
    return x * (1 + scale) + shift


def rope_cos_sin(pos, dim):
    inv = ROPE_BASE ** (-jnp.arange(0, dim, 2, dtype=jnp.float32) / dim)
    ang = pos.astype(jnp.float32)[:, None] * inv[None, :]
    return jnp.cos(ang), jnp.sin(ang)


def apply_rope(x, cos, sin):
    x1, x2 = jnp.split(x, 2, axis=-1)
    c = cos[:, None, :].astype(x.dtype)
    s = sin[:, None, :].astype(x.dtype)
    return jnp.concatenate([x1 * c - x2 * s, x1 * s + x2 * c], axis=-1)


def apply_axial_rope(x, row_pos, col_pos):
    half = x.shape[-1] // 2
    xr, xc = jnp.split(x, 2, axis=-1)
    rc, rs = rope_cos_sin(row_pos, half)
    cc, cs = rope_cos_sin(col_pos, half)
    return jnp.concatenate([apply_rope(xr, rc, rs), apply_rope(xc, cc, cs)], axis=-1)


def centred_depthwise_conv(x, w):
    K = w.shape[0]
    return lax.conv_general_dilated(
        x, w[:, None, :].astype(x.dtype), window_strides=(1,),
        padding=[((K - 1) // 2, K // 2)],
        dimension_numbers=('NWC', 'WIO', 'NWC'), feature_group_count=x.shape[-1])


def gated_delta_chunk(q, k, v, g, beta, s0):
    B, H, T, Dk = q.shape
    Dv = v.shape[-1]
    C = GDN_CHUNK
    N = T // C
    f32 = jnp.float32
    q, k, v = (a.astype(f32).reshape(B, H, N, C, a.shape[-1]) for a in (q, k, v))
    beta = beta.astype(f32).reshape(B, H, N, C)
    gc = jnp.cumsum(g.astype(f32).reshape(B, H, N, C), axis=-1)
    incl = jnp.tril(jnp.ones((C, C), dtype=bool))
    strict = jnp.tril(jnp.ones((C, C), dtype=bool), -1)
    decay = jnp.exp(jnp.where(incl, gc[..., :, None] - gc[..., None, :], -jnp.inf))
    kb = k * beta[..., None]
    vb = v * beta[..., None]
    lower = jnp.where(strict, jnp.einsum('bhncd,bhnsd->bhncs', kb, k) * decay, 0.0)
    eye = jnp.eye(C, dtype=f32)
    t_inv = lax.linalg.triangular_solve(eye + lower, jnp.broadcast_to(eye, lower.shape),
                                        left_side=True, lower=True)
    u = t_inv @ vb
    w = t_inv @ (kb * jnp.exp(gc)[..., None])
    attn = jnp.einsum('bhncd,bhnsd->bhncs', q, k) * decay
    q_dec = q * jnp.exp(gc)[..., None]
    k_dec = k * jnp.exp(gc[..., -1:] - gc)[..., None]
    g_tot = jnp.exp(gc[..., -1])

    def step(state, xs):
        u_n, w_n, a_n, qd_n, kd_n, gt_n = xs
        v_new = u_n - w_n @ state
        o_n = qd_n @ state + a_n @ v_new
        state = state * gt_n[..., None, None] + jnp.swapaxes(kd_n, -1, -2) @ v_new
        return state, o_n

    xs = tuple(jnp.moveaxis(a, 2, 0) for a in (u, w, attn, q_dec, k_dec, g_tot))
    s_fin, o = lax.scan(step, s0.astype(f32), xs)
    return jnp.moveaxis(o, 0, 2).reshape(B, H, T, Dv), s_fin


def retention_chunk(q, k, v, log_gamma, s0):
    B, H, T, Dk = q.shape
    Dv = v.shape[-1]
    C = RET_CHUNK
    N = T // C
    f32 = jnp.float32
    q, k, v = (a.astype(f32).reshape(B, H, N, C, a.shape[-1]) for a in (q, k, v))
    pos = jnp.arange(C, dtype=f32)
    lg = log_gamma.astype(f32)[:, None]
    incl = jnp.tril(jnp.ones((C, C), dtype=bool))
    decay = jnp.exp(jnp.where(incl, (pos[:, None] - pos[None, :]) * lg[..., None], -jnp.inf))
    q_dec = q * jnp.exp((pos + 1) * lg)[None, :, None, :, None]
    k_dec = k * jnp.exp((C - 1 - pos) * lg)[None, :, None, :, None]
    g_tot = jnp.exp(C * lg[:, 0])[None, :, None, None]
    o_intra = (jnp.einsum('bhncd,bhnsd->bhncs', q, k) * decay[None, :, None]) @ v

    def step(state, xs):
        qd_n, kd_n, v_n = xs
        o_n = qd_n @ state
        state = state * g_tot + jnp.swapaxes(kd_n, -1, -2) @ v_n
        return state, o_n

    xs = tuple(jnp.moveaxis(a, 2, 0) for a in (q_dec, k_dec, v))
    s_fin, o_inter = lax.scan(step, s0.astype(f32), xs)
    o = o_intra + jnp.moveaxis(o_inter, 0, 2)
    return o.reshape(B, H, T, Dv), s_fin


def two_way(run, ctx_seqs, lat_seqs, s0):
    flip = lambda seqs: tuple(jnp.flip(a, axis=2) for a in seqs)
    o_ctx_f, s_f = run(ctx_seqs, 0, s0)
    o_lat_f, _ = run(lat_seqs, 0, s_f)
    o_ctx_b, s_b = run(flip(ctx_seqs), 1, s0)
    o_lat_b, _ = run(flip(lat_seqs), 1, s_b)
    return o_ctx_f + jnp.flip(o_ctx_b, axis=2), o_lat_f + jnp.flip(o_lat_b, axis=2)


def gdn_inputs(qkv, a, b, conv_w, A_log, dt_bias):
    B, T, _ = qkv.shape
    qkv = jax.nn.silu(centred_depthwise_conv(qkv, conv_w))
    q, k, v = jnp.split(qkv, 3, axis=-1)
    heads = lambda x: jnp.swapaxes(x.reshape(B, T, GDN_HEADS, GDN_DIM), 1, 2)
    q = l2_norm(heads(q)) * GDN_DIM ** -0.5
    k = l2_norm(heads(k))
    v = heads(v)
    a = a.astype(jnp.float32).reshape(B, T, 2, GDN_HEADS)
    g = -jnp.exp(A_log.astype(jnp.float32)) * jax.nn.softplus(a + dt_bias.astype(jnp.float32))
    beta = jax.nn.sigmoid(b.astype(jnp.float32).reshape(B, T, 2, GDN_HEADS))
    return (q, k, v, jnp.transpose(g, (0, 3, 1, 2)), jnp.transpose(beta, (0, 3, 1, 2)))


def gdn_output(o, z, g):
    B, H, T, Dv = o.shape
    y = rms_norm(jnp.swapaxes(o, 1, 2), g) * jax.nn.silu(z.astype(jnp.float32)).reshape(B, T, H, Dv)
    return y.reshape(B, T, H * Dv).astype(z.dtype)


def ret_inputs(q, k, v, pos):
    B, T, _ = q.shape
    q = q.reshape(B, T, RET_HEADS, RET_DK)
    k = k.reshape(B, T, RET_HEADS, RET_DK)
    v = v.reshape(B, T, RET_HEADS, RET_DV)
    cos, sin = rope_cos_sin(pos, RET_DK)
    q = apply_rope(q, cos, sin)
    k = apply_rope(k, cos, sin) * RET_DK ** -0.5
    return (jnp.swapaxes(q, 1, 2), jnp.swapaxes(k, 1, 2), jnp.swapaxes(v, 1, 2))


def retention_output(o, gate, g):
    B, H, T, Dv = o.shape
    o = jnp.swapaxes(o, 1, 2)
    mu = jnp.mean(o, axis=-1, keepdims=True)
    var = jnp.mean(jnp.square(o - mu), axis=-1, keepdims=True)
    y = (o - mu) * lax.rsqrt(var + EPS) * g.astype(jnp.float32).reshape(H, Dv)
    return (y.reshape(B, T, H * Dv) * jax.nn.silu(gate.astype(jnp.float32))).astype(gate.dtype)


def centred_pool_minus_x(x, w):
    T = x.shape[1]
    xf = x.astype(jnp.float32)
    csum = jnp.pad(jnp.cumsum(xf, axis=1), ((0, 0), (1, 0), (0, 0)))
    t = jnp.arange(T)
    lo = jnp.clip(t - w // 2, 0, T)
    hi = jnp.clip(t + w // 2, 0, T)
    mean = (csum[:, hi] - csum[:, lo]) / (hi - lo).astype(jnp.float32)[None, :, None]
    return (mean - xf).astype(x.dtype)


def pool_mixer(x, pool_w, pool_scale):
    groups = jnp.split(x, len(POOL_WINDOWS), axis=-1)
    outs = [jnp.einsum('btc,ce->bte', centred_pool_minus_x(xg, w), pool_w[i])
            for i, (xg, w) in enumerate(zip(groups, POOL_WINDOWS))]
    return jnp.concatenate(outs, axis=-1) * pool_scale


def diff_attention(q_ctx, k_ctx, v_ctx, q_lat, k_lat, v_lat, lam):
    B = q_lat.shape[0]
    scale = DIFF_DK ** -0.5

    def attend(q, k, v):
        s = jnp.einsum('bqhd,bkhd->bhqk', q, k).astype(jnp.float32) * scale
        p = jax.nn.softmax(s, axis=-1).reshape(B, DIFF_HEADS, 2, q.shape[1], k.shape[1])
        a = (p[:, :, 0] - lam * p[:, :, 1]).astype(v.dtype)
        return jnp.einsum('bhqk,bkhe->bqhe', a, v)

    o_ctx = attend(q_ctx, k_ctx, v_ctx)
    k_all = jnp.concatenate([k_ctx, k_lat], axis=1)
    v_all = jnp.concatenate([v_ctx, v_lat], axis=1)
    S = q_lat.shape[1]
    nb = S // DIFF_QBLOCK
    qb = jnp.moveaxis(q_lat.reshape(B, nb, DIFF_QBLOCK, 2 * DIFF_HEADS, DIFF_DK), 1, 0)
    o_lat = lax.map(lambda qq: attend(qq, k_all, v_all), qb)
    o_lat = jnp.moveaxis(o_lat, 0, 1).reshape(B, S, DIFF_HEADS, DIFF_DV)
    return o_ctx, o_lat


def token_mixer(h_ctx, h_lat, w_in, gdn_conv, gdn_A_log, gdn_dt_bias, gdn_norm, ret_decay_exp, ret_norm,
                pool_w, pool_scale, diff_lambda, diff_norm, lam_init, row_pos, col_pos, ret_pos):
    B, L, _ = h_ctx.shape
    f32 = jnp.float32
    proj = jnp.einsum('btd,de->bte', jnp.concatenate([h_ctx, h_lat], axis=1), w_in)
    (gq, gk, gv, gz, ga, gb, rq, rk, rv, rg, pin, dq, dk, dv) = jnp.split(proj, SPLIT_POINTS, axis=-1)
    sp = lambda a: (a[:, :L], a[:, L:])

    qkv = jnp.concatenate([gq, gk, gv], axis=-1)
    seq_ctx = gdn_inputs(qkv[:, :L], ga[:, :L], gb[:, :L], gdn_conv, gdn_A_log, gdn_dt_bias)
    seq_lat = gdn_inputs(qkv[:, L:], ga[:, L:], gb[:, L:], gdn_conv, gdn_A_log, gdn_dt_bias)

    def gdn_run(seqs, d, s0):
        q, k, v, g, beta = seqs
        return gated_delta_chunk(q, k, v, g[..., d], beta[..., d], s0)

    s0 = jnp.zeros((B, GDN_HEADS, GDN_DIM, GDN_DIM), f32)
    go_ctx, go_lat = two_way(gdn_run, seq_ctx, seq_lat, s0)
    z_ctx, z_lat = sp(gz)
    gdn_ctx, gdn_lat = gdn_output(go_ctx, z_ctx, gdn_norm), gdn_output(go_lat, z_lat, gdn_norm)

    log_gamma = jnp.log1p(-jnp.exp2(-ret_decay_exp.astype(f32)))
    rseq_ctx = ret_inputs(rq[:, :L], rk[:, :L], rv[:, :L], ret_pos[:L])
    rseq_lat = ret_inputs(rq[:, L:], rk[:, L:], rv[:, L:], ret_pos[L:])

    def ret_run(seqs, d, s0):
        q, k, v = seqs
        return retention_chunk(q, k, v, log_gamma[d], s0)

    r0 = jnp.zeros((B, RET_HEADS, RET_DK, RET_DV), f32)
    ro_ctx, ro_lat = two_way(ret_run, rseq_ctx, rseq_lat, r0)
    rg_ctx, rg_lat = sp(rg)
    ret_ctx, ret_lat = retention_output(ro_ctx, rg_ctx, ret_norm), retention_output(ro_lat, rg_lat, ret_norm)

    p_ctx, p_lat = sp(pin)
    pool_ctx, pool_lat = pool_mixer(p_ctx, pool_w, pool_scale), pool_mixer(p_lat, pool_w, pool_scale)

    heads_qk = lambda a: a.reshape(B, a.shape[1], 2 * DIFF_HEADS, DIFF_DK)
    heads_v = lambda a: a.reshape(B, a.shape[1], DIFF_HEADS, DIFF_DV)
    dq_c, dq_l = sp(dq)
    dk_c, dk_l = sp(dk)
    dv_c, dv_l = sp(dv)
    dq_l = apply_axial_rope(heads_qk(dq_l), row_pos, col_pos)
    dk_l = apply_axial_rope(heads_qk(dk_l), row_pos, col_pos)
    lq1, lk1, lq2, lk2 = diff_lambda.astype(f32)
    lam = jnp.exp(jnp.sum(lq1 * lk1)) - jnp.exp(jnp.sum(lq2 * lk2)) + lam_init
    do_ctx, do_lat = diff_attention(heads_qk(dq_c), heads_qk(dk_c), heads_v(dv_c), dq_l, dk_l, heads_v(dv_l), lam)
    diff_ctx = (rms_norm(do_ctx, diff_norm) * (1.0 - lam_init)).reshape(B, L, GROUP_WIDTH)
    diff_lat = (rms_norm(do_lat, diff_norm) * (1.0 - lam_init)).reshape(B, do_lat.shape[1], GROUP_WIDTH)

    y_ctx = jnp.concatenate([gdn_ctx, ret_ctx, pool_ctx, diff_ctx], axis=-1)
    y_lat = jnp.concatenate([gdn_lat, ret_lat, pool_lat, diff_lat], axis=-1)
    return y_ctx, y_lat


def expert_choice_ffn(h, w_router, w_gate, w_up, w_down):
    B, T, D = h.shape
    cap = EC_CAPACITY * T // N_EXPERTS
    aff = jax.nn.softmax(jnp.einsum('btd,de->bte', h, w_router).astype(jnp.float32), axis=-1)
    gates, idx = lax.top_k(jnp.swapaxes(aff, 1, 2), cap)
    xs = jax.vmap(lambda hb, ib: hb[ib])(h, idx)
    a = jnp.einsum('becd,edf->becf', xs, w_gate)
    u = jnp.einsum('becd,edf->becf', xs, w_up)
    y = jnp.einsum('becf,efd->becd', jax.nn.silu(a) * u, w_down) * gates[..., None].astype(h.dtype)
    return jax.vmap(lambda yb, ib: jnp.zeros((T, D), yb.dtype).at[ib.reshape(-1)].add(yb.reshape(-1, D)))(y, idx)


def setup_inputs(seed: int = 0) -> dict:
    key = jax.random.key(seed)
    ks = jax.random.split(key, 26)
    f32 = jnp.float32
    nrm = lambda k, shape, scale: jax.random.normal(k, shape, f32) * scale
    D = D_MODEL
    dt = jnp.exp(jax.random.uniform(ks[11], (DEPTH, 2, GDN_HEADS), f32, math.log(1e-3), math.log(1e-1)))
    return {
        'x': nrm(ks[0], (BATCH, SEQ, D), 1.0),
        'c': nrm(ks[1], (BATCH, D), 1.0),
        'ctx': nrm(ks[2], (BATCH, CTX_LEN, D), 1.0),
        'c_ctx': nrm(ks[3], (D,), 1.0),
        'w_ada': nrm(ks[4], (DEPTH, D, ADA_CHUNKS * D), 0.5 * D ** -0.5),
        'b_ada': nrm(ks[5], (DEPTH, ADA_CHUNKS * D), 0.02),
        'norm_mix': 1.0 + nrm(ks[6], (DEPTH, D), 0.02),
        'norm_ffn': 1.0 + nrm(ks[7], (DEPTH, D), 0.02),
        'w_in': nrm(ks[8], (DEPTH, D, IN_WIDTH), D ** -0.5),
        'gdn_conv': nrm(ks[9], (DEPTH, GDN_CONV, 3 * GROUP_WIDTH), GDN_CONV ** -0.5),
        'gdn_A_log': jnp.log(jax.random.uniform(ks[10], (DEPTH, 2, GDN_HEADS), f32, 1.0, 16.0)),
        'gdn_dt_bias': dt + jnp.log(-jnp.expm1(-dt)),
        'gdn_norm': 1.0 + nrm(ks[12], (DEPTH, GDN_DIM), 0.02),
        'ret_decay_exp': RET_MIN_EXP + jnp.arange(RET_HEADS, dtype=f32) + nrm(ks[13], (DEPTH, 2, RET_HEADS), 0.1),
        'ret_norm': 1.0 + nrm(ks[14], (DEPTH, GROUP_WIDTH), 0.02),
        'pool_w': nrm(ks[15], (DEPTH, len(POOL_WINDOWS), POOL_GROUP, POOL_GROUP), POOL_GROUP ** -0.5),
        'pool_scale': 1.0 + nrm(ks[16], (DEPTH, GROUP_WIDTH), 0.02),
        'diff_lambda': nrm(ks[17], (DEPTH, 4, DIFF_DK), 0.1),
        'diff_norm': 1.0 + nrm(ks[18], (DEPTH, DIFF_DV), 0.02),
        'w_out': nrm(ks[19], (DEPTH, MIX_WIDTH, D), MIX_WIDTH ** -0.5),
        'w_router': nrm(ks[20], (DEPTH, D, N_EXPERTS), D ** -0.5),
        'w_gate': nrm(ks[21], (DEPTH, N_EXPERTS, D, EXPERT_FF), D ** -0.5),
        'w_up': nrm(ks[22], (DEPTH, N_EXPERTS, D, EXPERT_FF), D ** -0.5),
        'w_down': nrm(ks[23], (DEPTH, N_EXPERTS, EXPERT_FF, D), EXPERT_FF ** -0.5),
        'final_norm': 1.0 + nrm(ks[24], (D,), 0.02),
    }


def reference(x, c, ctx, c_ctx, w_ada, b_ada, norm_mix, norm_ffn, w_in, gdn_conv, gdn_A_log, gdn_dt_bias,
              gdn_norm, ret_decay_exp, ret_norm, pool_w, pool_scale, diff_lambda, diff_norm, w_out,
              w_router, w_gate, w_up, w_down, final_norm):
    S = x.shape[1]
    L = ctx.shape[1]
    rows = S // GRID_W
    row_pos = jnp.repeat(jnp.arange(rows), GRID_W)
    col_pos = jnp.tile(jnp.arange(GRID_W), rows)
    ret_pos = jnp.arange(L + S)
    silu_c = jax.nn.silu(c)
    silu_cc = jax.nn.silu(c_ctx)
    x_lat, x_ctx = x, ctx
    for i in range(DEPTH):
        lam_init = 0.8 - 0.6 * math.exp(-0.3 * i)
        mod_lat = (silu_c @ w_ada[i] + b_ada[i])[:, None, :]
        mod_ctx = (silu_cc @ w_ada[i] + b_ada[i])[None, None, :]
        sh1, sc1, gt1, sh2, sc2, gt2 = jnp.split(mod_lat, ADA_CHUNKS, axis=-1)
        csh1, csc1, cgt1, csh2, csc2, cgt2 = jnp.split(mod_ctx, ADA_CHUNKS, axis=-1)

        h_lat = modulate(rms_norm(x_lat, norm_mix[i]), sh1, sc1)
        h_ctx = modulate(rms_norm(x_ctx, norm_mix[i]), csh1, csc1)
        y_ctx, y_lat = token_mixer(h_ctx, h_lat, w_in[i], gdn_conv[i], gdn_A_log[i], gdn_dt_bias[i], gdn_norm[i],
                                   ret_decay_exp[i], ret_norm[i], pool_w[i], pool_scale[i], diff_lambda[i],
                                   diff_norm[i], lam_init, row_pos, col_pos, ret_pos)
        x_lat = x_lat + gt1 * jnp.einsum('btm,md->btd', y_lat, w_out[i])
        h_lat = modulate(rms_norm(x_lat, norm_ffn[i]), sh2, sc2)
        x_lat = x_lat + gt2 * expert_choice_ffn(h_lat, w_router[i], w_gate[i], w_up[i], w_down[i])

        if i < DEPTH - 1:
            x_ctx = x_ctx + cgt1 * jnp.einsum('btm,md->btd', y_ctx, w_out[i])
            h_ctx = modulate(rms_norm(x_ctx, norm_ffn[i]), csh2, csc2)
            x_ctx = x_ctx + cgt2 * expert_choice_ffn(h_ctx, w_router[i], w_gate[i], w_up[i], w_down[i])
    return rms_norm(x_lat, final_norm)
```

```python
import functools
import math

import jax
import jax.numpy as jnp
from jax import lax
from jax.experimental import pallas as pl
from jax.experimental.pallas import tpu as pltpu

F32 = jnp.float32
BF16 = jnp.bfloat16
HIGHEST = lax.Precision.HIGHEST

D_MODEL = 4096
GROUP_W = 1024
N_HEADS = 8
HEAD_DV = 128
HEAD_DK = 64
ADA_CHUNKS = 6
EPS = 1e-6
ROPE_BASE = 10000.0
GDN_CONV_K = 5
GDN_CHUNK = 64
RET_CHUNK = 256
POOL_WINDOWS = (2, 4, 8, 16)
POOL_GROUP = 256
N_EXPERTS = 16
EXPERT_FF = 256
EC_CAPACITY = 2
GRID_W = 64
IN_AB0, IN_AB1 = 4096, 4128
MAIN_W = 11264
OFF_GQ, OFF_GK, OFF_GV, OFF_GZ = 0, 1024, 2048, 3072
OFF_RQ, OFF_RK, OFF_RV, OFF_RG = 4096, 4608, 5120, 6144
OFF_POOL = 7168
OFF_DQ, OFF_DK, OFF_DV = 8192, 9216, 10240

LANES = 128
ROW_TILE = 256
VMEM_LIMIT_BYTES = 58 * 1024 * 1024


def _cparams(n_axes):
    return pltpu.CompilerParams(dimension_semantics=("arbitrary",) * n_axes,
                                vmem_limit_bytes=VMEM_LIMIT_BYTES)


def _silu(x):
    return x / (1.0 + jnp.exp(-x))


def _sigmoid(x):
    return 1.0 / (1.0 + jnp.exp(-x))


def _dot(a, b):
    return jnp.dot(a, b, preferred_element_type=F32)


def _dot_nt(a, b):
    return lax.dot_general(a, b, (((1,), (1,)), ((), ())), preferred_element_type=F32)


def _dot_tn(a, b):
    return lax.dot_general(a, b, (((0,), (0,)), ((), ())), preferred_element_type=F32)


def _dot_hi(a, b):
    return jnp.dot(a, b, precision=HIGHEST, preferred_element_type=F32)


def _swap_halves(x, group):
    half = group // 2
    lane = lax.broadcasted_iota(jnp.int32, x.shape, x.ndim - 1)
    return jnp.where((lane % group) < half,
                     pltpu.roll(x, LANES - half, axis=x.ndim - 1),
                     pltpu.roll(x, half, axis=x.ndim - 1))


def _ada_kernel(c_ref, w_ref, b_ref, o_ref):
    s = _silu(c_ref[...]).astype(BF16)
    o_ref[...] = _dot(s, w_ref[...].astype(BF16)) + b_ref[...]


def _ada(c_all, w_ada, b_ada):
    depth, d, n = w_ada.shape
    tn = 512
    return pl.pallas_call(
        _ada_kernel,
        grid=(depth, n // tn),
        in_specs=[pl.BlockSpec((8, d), lambda l, j: (0, 0)),
                  pl.BlockSpec((None, d, tn), lambda l, j: (l, 0, j)),
                  pl.BlockSpec((None, 1, tn), lambda l, j: (l, 0, j))],
        out_specs=pl.BlockSpec((None, 8, tn), lambda l, j: (l, 0, j)),
        out_shape=jax.ShapeDtypeStruct((depth, 8, n), F32),
        compiler_params=_cparams(2),
        name="ada",
    )(c_all, w_ada, b_ada.reshape(depth, 1, n))


def _mod_row(i, tiles_per_sample):
    return jnp.where(i % tiles_per_sample == 0, 0, 1 + i // tiles_per_sample)


def _normmod_value(x_ref, g_ref, sh_ref, sc_ref):
    x = x_ref[...]
    ms = jnp.mean(x * x, axis=-1, keepdims=True)
    y = x * lax.rsqrt(ms + EPS) * g_ref[...]
    return y * (1.0 + sc_ref[...]) + sh_ref[...]


def _normmod_kernel(x_ref, g_ref, sh_ref, sc_ref, h_ref):
    h_ref[...] = _normmod_value(x_ref, g_ref, sh_ref, sc_ref).astype(h_ref.dtype)


def _normmod_router_kernel(x_ref, g_ref, sh_ref, sc_ref, wr_ref, h_ref, aff_ref):
    h = _normmod_value(x_ref, g_ref, sh_ref, sc_ref)
    h_ref[...] = h.astype(h_ref.dtype)
    logits = _dot_hi(h, wr_ref[...])
    lane = lax.broadcasted_iota(jnp.int32, logits.shape, 1)
    logits = jnp.where(lane < N_EXPERTS, logits, -jnp.inf)
    e = jnp.exp(logits - jnp.max(logits, axis=-1, keepdims=True))
    aff_ref[...] = e / jnp.sum(e, axis=-1, keepdims=True)


def _normmod(x, g, modl, shift_chunk, t_all, w_router=None):
    r, d = x.shape
    tm = ROW_TILE
    tps = t_all // tm
    mod_spec = lambda chunk: pl.BlockSpec(
        (None, None, 1, d), lambda i: (_mod_row(i, tps), chunk, 0, 0))
    in_specs = [pl.BlockSpec((tm, d), lambda i: (i, 0)),
                pl.BlockSpec((1, d), lambda i: (0, 0)),
                mod_spec(shift_chunk), mod_spec(shift_chunk + 1)]
    h_spec = pl.BlockSpec((tm, d), lambda i: (i, 0))
    h_shape = jax.ShapeDtypeStruct((r, d), BF16 if w_router is None else F32)
    if w_router is None:
        return pl.pallas_call(
            _normmod_kernel, grid=(r // tm,), in_specs=in_specs, out_specs=h_spec,
            out_shape=h_shape, compiler_params=_cparams(1), name="normmod",
        )(x, g.reshape(1, d), modl, modl)
    wr = jnp.pad(w_router, ((0, 0), (0, LANES - N_EXPERTS)))
    return pl.pallas_call(
        _normmod_router_kernel, grid=(r // tm,),
        in_specs=in_specs + [pl.BlockSpec((d, LANES), lambda i: (0, 0))],
        out_specs=[h_spec, pl.BlockSpec((tm, LANES), lambda i: (i, 0))],
        out_shape=[h_shape, jax.ShapeDtypeStruct((r, LANES), F32)],
        compiler_params=_cparams(1), name="normmod_router",
    )(x, g.reshape(1, d), modl, modl, wr)


def _mm_kernel(x_ref, w_ref, o_ref):
    o_ref[...] = _dot(x_ref[...], w_ref[...]).astype(o_ref.dtype)


def _matmul(x, w, tm, tn, name):
    m, k = x.shape
    n = w.shape[1]
    return pl.pallas_call(
        _mm_kernel, grid=(m // tm, n // tn),
        in_specs=[pl.BlockSpec((tm, k), lambda i, j: (i, 0)),
                  pl.BlockSpec((k, tn), lambda i, j: (0, j))],
        out_specs=pl.BlockSpec((tm, tn), lambda i, j: (i, j)),
        out_shape=jax.ShapeDtypeStruct((m, n), F32),
        compiler_params=_cparams(2), name=name,
    )(x, w)


def _mm_res_kernel(y_ref, w_ref, x_ref, gs_ref, gc_ref, o_ref, *, tm, tiles_per_sample, l_ctx):
    acc = _dot(y_ref[...], w_ref[...])
    i = pl.program_id(0)
    row = (i % tiles_per_sample) * tm + lax.broadcasted_iota(jnp.int32, (tm, 1), 0)
    gate = jnp.where(row < l_ctx, gc_ref[...], gs_ref[...])
    o_ref[...] = x_ref[...] + gate * acc


def _out_proj(y, w, x, modl, gate_chunk, t_all, l_ctx, tm, tn):
    m, k = y.shape
    n = w.shape[1]
    tps = t_all // tm
    kern = functools.partial(_mm_res_kernel, tm=tm, tiles_per_sample=tps, l_ctx=l_ctx)
    return pl.pallas_call(
        kern, grid=(m // tm, n // tn),
        in_specs=[pl.BlockSpec((tm, k), lambda i, j: (i, 0)),
                  pl.BlockSpec((k, tn), lambda i, j: (0, j)),
                  pl.BlockSpec((tm, tn), lambda i, j: (i, j)),
                  pl.BlockSpec((None, None, 1, tn), lambda i, j: (1 + i // tps, gate_chunk, 0, j)),
                  pl.BlockSpec((None, None, 1, tn), lambda i, j: (0, gate_chunk, 0, j))],
        out_specs=pl.BlockSpec((tm, tn), lambda i, j: (i, j)),
        out_shape=jax.ShapeDtypeStruct((m, n), F32),
        compiler_params=_cparams(2), name="out_proj",
    )(y, w, x, modl, modl)


def _gdn_gate_kernel(ab_ref, alog_ref, dtb_ref, o_ref):
    ab = ab_ref[...]
    lane = lax.broadcasted_iota(jnp.int32, (1, LANES), 1)
    z = ab + dtb_ref[...]
    softplus = jnp.maximum(z, 0.0) + jnp.log1p(jnp.exp(-jnp.abs(z)))
    g = -jnp.exp(alog_ref[...]) * softplus
    beta = _sigmoid(ab)
    c = GDN_CHUNK
    ii = lax.broadcasted_iota(jnp.int32, (c, c), 0)
    jj = lax.broadcasted_iota(jnp.int32, (c, c), 1)
    ltri = jnp.where(ii >= jj, 1.0, 0.0).astype(F32)
    for n in range(ab.shape[0] // c):
        gk = g[n * c:(n + 1) * c]
        pre = _dot_hi(ltri, gk)
        tot = pre[c - 1:c, :]
        suf = tot - pre + gk
        gc = jnp.where(lane < N_HEADS, pre, suf)
        e1 = jnp.exp(gc)
        e2 = jnp.exp(tot - gc)
        gt = jnp.broadcast_to(jnp.exp(tot), gc.shape)
        out = jnp.where(lane < 16, gc,
              jnp.where(lane < 32, beta[n * c:(n + 1) * c],
              jnp.where(lane < 48, pltpu.roll(e1, 32, axis=1),
              jnp.where(lane < 64, pltpu.roll(e2, 48, axis=1),
              jnp.where(lane < 80, pltpu.roll(gt, 64, axis=1), 0.0)))))
        o_ref[n * c:(n + 1) * c, :] = out


def _gdn_gates(ab, a_log, dt_bias):
    r = ab.shape[0]
    tm = ROW_TILE
    pad = lambda p: jnp.pad(p.reshape(1, 2 * N_HEADS), ((0, 0), (0, LANES - 2 * N_HEADS)))
    return pl.pallas_call(
        _gdn_gate_kernel, grid=(r // tm,),
        in_specs=[pl.BlockSpec((tm, LANES), lambda i: (i, 0)),
                  pl.BlockSpec((1, LANES), lambda i: (0, 0)),
                  pl.BlockSpec((1, LANES), lambda i: (0, 0))],
        out_specs=pl.BlockSpec((tm, LANES), lambda i: (i, 0)),
        out_shape=jax.ShapeDtypeStruct((r, LANES), F32),
        compiler_params=_cparams(1), name="gdn_gates",
    )(ab, pad(a_log), pad(dt_bias))


CONV_PAD = 8


def _gdn_kernel(q_ref, k_ref, v_ref, z_ref, wq_ref, wk_ref, wv_ref, col_ref, row_ref, gn_ref,
                o_ref, qn, kn, vn, pad, oacc, *, l_ctx):
    t_all = q_ref.shape[0]
    c = GDN_CHUNK
    n_chunks = t_all // c
    nc_ctx = l_ctx // c
    rt = ROW_TILE

    def conv_into(src_ref, w_ref, dst, l2, scale):
        w = w_ref[...]
        for (r0, n) in ((0, l_ctx), (l_ctx, t_all - l_ctx)):
            pad[0:CONV_PAD, :] = jnp.zeros((CONV_PAD, LANES), F32)
            pad[CONV_PAD:CONV_PAD + n, :] = src_ref[r0:r0 + n, :]
            pad[CONV_PAD + n:2 * CONV_PAD + n, :] = jnp.zeros((CONV_PAD, LANES), F32)
            for t0 in range(0, n, rt):
                base = CONV_PAD - (GDN_CONV_K - 1) // 2 + t0
                acc = w[0:1, :] * pad[base:base + rt, :]
                for j in range(1, GDN_CONV_K):
                    acc = acc + w[j:j + 1, :] * pad[base + j:base + j + rt, :]
                s = _silu(acc)
                if l2:
                    s = s * lax.rsqrt(jnp.sum(s * s, axis=-1, keepdims=True) + EPS) * scale
                dst[r0 + t0:r0 + t0 + rt, :] = s

    conv_into(q_ref, wq_ref, qn, True, HEAD_DV ** -0.5)
    conv_into(k_ref, wk_ref, kn, True, 1.0)
    conv_into(v_ref, wv_ref, vn, False, 1.0)
    oacc[...] = jnp.zeros(oacc.shape, F32)

    ii = lax.broadcasted_iota(jnp.int32, (c, c), 0)
    jj = lax.broadcasted_iota(jnp.int32, (c, c), 1)
    eye = jnp.where(ii == jj, 1.0, 0.0).astype(F32)

    def chunk(d, ci, state):
        r0 = pl.multiple_of(ci * c, c)
        q = qn[pl.ds(r0, c), :]
        k = kn[pl.ds(r0, c), :]
        v = vn[pl.ds(r0, c), :]
        col = col_ref[pl.ds(r0, c), :]
        gc = col[:, d:d + 1]
        beta = col[:, 2 + d:3 + d]
        e1 = col[:, 4 + d:5 + d]
        e2 = col[:, 6 + d:7 + d]
        gt = col[0:1, 8 + d:9 + d]
        gcr = row_ref[ci, pl.ds(d, 1), :]
        incl = (ii >= jj) if d == 0 else (ii <= jj)
        strict = (ii > jj) if d == 0 else (ii < jj)
        dm = jnp.exp(jnp.where(incl, gc - gcr, -jnp.inf))
        k16 = k.astype(BF16)
        kk = _dot_nt(k16, k16)
        qk = _dot_nt(q.astype(BF16), k16)
        p = -jnp.where(strict, beta * kk * dm, 0.0)
        t = eye + p
        for _ in range(5):
            p = _dot_hi(p, p)
            t = t + _dot_hi(t, p)
        t16 = t.astype(BF16)
        u = _dot(t16, (v * beta).astype(BF16))
        w = _dot(t16, (k * (beta * e1)).astype(BF16))
        s16 = state.astype(BF16)
        v_new = u - _dot(w.astype(BF16), s16)
        vn16 = v_new.astype(BF16)
        o = _dot((q * e1).astype(BF16), s16) + _dot((qk * dm).astype(BF16), vn16)
        new_state = state * gt + _dot_tn((k * e2).astype(BF16), vn16)
        oacc[pl.ds(r0, c), :] += o
        return new_state

    def body(i, carry):
        sf, sb = carry
        cb = jnp.where(i < nc_ctx, nc_ctx - 1 - i, n_chunks - 1 - i + nc_ctx)
        sf = chunk(0, i, sf)
        sb = chunk(1, cb, sb)
        return sf, sb

    zero = jnp.zeros((HEAD_DV, HEAD_DV), F32)
    lax.fori_loop(0, n_chunks, body, (zero, zero))

    def post(i, carry):
        r0 = pl.multiple_of(i * rt, rt)
        o = oacc[pl.ds(r0, rt), :]
        y = o * lax.rsqrt(jnp.mean(o * o, axis=-1, keepdims=True) + EPS) * gn_ref[...]
        o_ref[pl.ds(r0, rt), :] = (y * _silu(z_ref[pl.ds(r0, rt), :])).astype(o_ref.dtype)
        return carry

    lax.fori_loop(0, t_all // rt, post, 0)


def _gdn(proj, colh, rowh, conv_w, gdn_norm, n_batch, t_all, l_ctx):
    r = proj.shape[0]
    cb = lambda off: (lambda b, h: (b, off // LANES + h))
    wb = lambda off: (lambda b, h: (0, off // LANES + h))
    seq = lambda im: pl.BlockSpec((t_all, LANES), im)
    nch = t_all // GDN_CHUNK
    return pl.pallas_call(
        functools.partial(_gdn_kernel, l_ctx=l_ctx),
        grid=(n_batch, N_HEADS),
        in_specs=[seq(cb(OFF_GQ)), seq(cb(OFF_GK)), seq(cb(OFF_GV)), seq(cb(OFF_GZ)),
                  pl.BlockSpec((8, LANES), wb(0)), pl.BlockSpec((8, LANES), wb(GROUP_W)),
                  pl.BlockSpec((8, LANES), wb(2 * GROUP_W)),
                  pl.BlockSpec((None, t_all, 16), lambda b, h: (h, b, 0)),
                  pl.BlockSpec((None, nch, 2, GDN_CHUNK), lambda b, h: (h, b, 0, 0)),
                  pl.BlockSpec((1, LANES), lambda b, h: (0, 0))],
        out_specs=pl.BlockSpec((t_all, LANES), lambda b, h: (b, h)),
        out_shape=jax.ShapeDtypeStruct((r, GROUP_W), BF16),
        scratch_shapes=[pltpu.VMEM((t_all, LANES), F32), pltpu.VMEM((t_all, LANES), F32),
                        pltpu.VMEM((t_all, LANES), F32),
                        pltpu.VMEM((t_all + 2 * CONV_PAD, LANES), F32),
                        pltpu.VMEM((t_all, LANES), F32)],
        compiler_params=_cparams(2), name="gdn",
    )(proj, proj, proj, proj, conv_w, conv_w, conv_w, colh, rowh, gdn_norm.reshape(1, LANES))


def _ret_kernel(q_ref, k_ref, v_ref, g_ref, cos_ref, sin_ref, rde_ref, rn_ref, o_ref,
                oacc, dmat, qdec, kdec, gtm, *, l_ctx):
    t_all = q_ref.shape[0]
    c = RET_CHUNK
    n_chunks = t_all // c
    pair = pl.program_id(1)
    lane = lax.broadcasted_iota(jnp.int32, (1, LANES), 1)
    lgv = jnp.log1p(-jnp.exp2(-rde_ref[...]))

    ii = lax.broadcasted_iota(jnp.int32, (c, c), 0)
    jj = lax.broadcasted_iota(jnp.int32, (c, c), 1)
    pos = lax.broadcasted_iota(jnp.int32, (c, LANES), 0).astype(F32)
    lane_c = lax.broadcasted_iota(jnp.int32, (c, LANES), 1)
    srow = lax.broadcasted_iota(jnp.int32, (LANES, 2 * HEAD_DV), 0)
    scol = lax.broadcasted_iota(jnp.int32, (LANES, 2 * HEAD_DV), 1)
    for d in range(2):
        lg = [jnp.sum(jnp.where(lane == 2 * pair + a, lgv[d:d + 1, :], 0.0), axis=1, keepdims=True)
              for a in range(2)]
        for a in range(2):
            dist = (ii - jj) if d == 0 else (jj - ii)
            dmat[a * 2 + d] = jnp.exp(jnp.where(dist >= 0, dist.astype(F32) * lg[a], -jnp.inf))
        lg_l = jnp.where(lane_c < HEAD_DK, lg[0], lg[1])
        qpow = (pos + 1.0) if d == 0 else (c - pos)
        kpow = (c - 1.0 - pos) if d == 0 else pos
        qdec[d] = jnp.exp(qpow * lg_l)
        kdec[d] = jnp.exp(kpow * lg_l)
        blk0 = (srow < HEAD_DK) & (scol < HEAD_DV)
        blk1 = (srow >= HEAD_DK) & (scol >= HEAD_DV)
        gtm[d] = jnp.where(blk0, jnp.exp(c * lg[0]), jnp.where(blk1, jnp.exp(c * lg[1]), 0.0))
    blockmask = jnp.where(((srow < HEAD_DK) & (scol < HEAD_DV)) | ((srow >= HEAD_DK) & (scol >= HEAD_DV)),
                          1.0, 0.0).astype(F32)
    oacc[...] = jnp.zeros(oacc.shape, F32)

    def chunk(d, ci, state):
        r0 = pl.multiple_of(ci * c, c)
        cos = cos_ref[pl.ds(r0, c), :]
        sin = sin_ref[pl.ds(r0, c), :]
        q = q_ref[pl.ds(r0, c), :]
        k = k_ref[pl.ds(r0, c), :]
        q = q * cos + _swap_halves(q, HEAD_DK) * sin
        k = (k * cos + _swap_halves(k, HEAD_DK) * sin) * HEAD_DK ** -0.5
        v16 = v_ref[pl.ds(r0, c), :].astype(BF16)
        k16 = k.astype(BF16)
        parts = []
        for a in range(2):
            qa = jnp.where((lane_c < HEAD_DK) == (a == 0), q, 0.0).astype(BF16)
            att = (_dot_nt(qa, k16) * dmat[a * 2 + d]).astype(BF16)
            parts.append(_dot(att, v16[:, a * HEAD_DV:(a + 1) * HEAD_DV]))
        s16 = state.astype(BF16)
        o = jnp.concatenate(parts, axis=1) + _dot((q * qdec[d]).astype(BF16), s16)
        new_state = state * gtm[d] + blockmask * _dot_tn((k * kdec[d]).astype(BF16), v16)
        oacc[pl.ds(r0, c), :] += o
        return new_state

    nc_ctx = l_ctx // c

    def body(i, carry):
        sf, sb = carry
        cb = jnp.where(i < nc_ctx, nc_ctx - 1 - i, n_chunks - 1 - i + nc_ctx)
        sf = chunk(0, i, sf)
        sb = chunk(1, cb, sb)
        return sf, sb

    zero = jnp.zeros((LANES, 2 * HEAD_DV), F32)
    lax.fori_loop(0, n_chunks, body, (zero, zero))

    def post(i, carry):
        r0 = pl.multiple_of(i * c, c)
        o = oacc[pl.ds(r0, c), :]
        gate = g_ref[pl.ds(r0, c), :]
        ys = []
        for a in range(2):
            oa = o[:, a * HEAD_DV:(a + 1) * HEAD_DV]
            mu = jnp.mean(oa, axis=-1, keepdims=True)
            var = jnp.mean(jnp.square(oa - mu), axis=-1, keepdims=True)
            ys.append((oa - mu) * lax.rsqrt(var + EPS) * rn_ref[:, a * HEAD_DV:(a + 1) * HEAD_DV])
        y = jnp.concatenate(ys, axis=1) * _silu(gate)
        o_ref[pl.ds(r0, c), :] = y.astype(o_ref.dtype)
        return carry

    lax.fori_loop(0, n_chunks, post, 0)


def _retention(proj, cos_t, sin_t, ret_decay_exp, ret_norm, n_batch, t_all, l_ctx):
    r = proj.shape[0]
    c = RET_CHUNK
    rde = jnp.pad(ret_decay_exp, ((0, 6), (0, LANES - N_HEADS)), constant_values=8.0)
    w2 = 2 * HEAD_DV
    return pl.pallas_call(
        functools.partial(_ret_kernel, l_ctx=l_ctx),
        grid=(n_batch, N_HEADS // 2),
        in_specs=[pl.BlockSpec((t_all, LANES), lambda b, p: (b, OFF_RQ // LANES + p)),
                  pl.BlockSpec((t_all, LANES), lambda b, p: (b, OFF_RK // LANES + p)),
                  pl.BlockSpec((t_all, w2), lambda b, p: (b, OFF_RV // w2 + p)),
                  pl.BlockSpec((t_all, w2), lambda b, p: (b, OFF_RG // w2 + p)),
                  pl.BlockSpec((t_all, LANES), lambda b, p: (0, 0)),
                  pl.BlockSpec((t_all, LANES), lambda b, p: (0, 0)),
                  pl.BlockSpec((8, LANES), lambda b, p: (0, 0)),
                  pl.BlockSpec((1, w2), lambda b, p: (0, p))],
        out_specs=pl.BlockSpec((t_all, w2), lambda b, p: (b, p)),
        out_shape=jax.ShapeDtypeStruct((r, GROUP_W), BF16),
        scratch_shapes=[pltpu.VMEM((t_all, w2), F32),
                        pltpu.VMEM((4, c, c), F32),
                        pltpu.VMEM((2, c, LANES), F32), pltpu.VMEM((2, c, LANES), F32),
                        pltpu.VMEM((2, LANES, w2), F32)],
        compiler_params=_cparams(2), name="retention",
    )(proj, proj, proj, proj, cos_t, sin_t, rde, ret_norm.reshape(1, GROUP_W))


POOL_PAD = 8


def _pool_kernel(x_ref, w_ref, sc_ref, o_ref, pad, *, l_ctx):
    t_all = x_ref.shape[0]
    rt = ROW_TILE
    grp = pl.program_id(1)
    w16 = w_ref[...].astype(BF16)
    for gi, win in enumerate(POOL_WINDOWS):
        @pl.when(grp == gi)
        def _(win=win):
            half = win // 2
            for (r0, n) in ((0, l_ctx), (l_ctx, t_all - l_ctx)):
                pad[0:POOL_PAD, :] = jnp.zeros((POOL_PAD, POOL_GROUP), F32)
                pad[POOL_PAD:POOL_PAD + n, :] = x_ref[r0:r0 + n, :]
                pad[POOL_PAD + n:2 * POOL_PAD + n, :] = jnp.zeros((POOL_PAD, POOL_GROUP), F32)

                for t0 in range(0, n, rt):
                    acc = pad[t0 + POOL_PAD - half:t0 + POOL_PAD - half + rt, :]
                    for j in range(1, win):
                        acc = acc + pad[t0 + POOL_PAD - half + j:t0 + POOL_PAD - half + j + rt, :]
                    t = t0 + lax.broadcasted_iota(jnp.int32, (rt, 1), 0)
                    cnt = jnp.minimum(t + half, n) - jnp.maximum(t - half, 0)
                    x = pad[t0 + POOL_PAD:t0 + POOL_PAD + rt, :]
                    dlt = (acc / cnt.astype(F32) - x).astype(BF16)
                    o_ref[r0 + t0:r0 + t0 + rt, :] = (_dot(dlt, w16) * sc_ref[...]).astype(o_ref.dtype)


def _pool(proj, pool_w, pool_scale, n_batch, t_all, l_ctx):
    r = proj.shape[0]
    g = POOL_GROUP
    return pl.pallas_call(
        functools.partial(_pool_kernel, l_ctx=l_ctx),
        grid=(n_batch, len(POOL_WINDOWS)),
        in_specs=[pl.BlockSpec((t_all, g), lambda b, i: (b, OFF_POOL // g + i)),
                  pl.BlockSpec((None, g, g), lambda b, i: (i, 0, 0)),
                  pl.BlockSpec((1, g), lambda b, i: (0, i))],
        out_specs=pl.BlockSpec((t_all, g), lambda b, i: (b, i)),
        out_shape=jax.ShapeDtypeStruct((r, GROUP_W), BF16),
        scratch_shapes=[pltpu.VMEM((t_all + 2 * POOL_PAD, g), F32)],
        compiler_params=_cparams(2), name="pool",
    )(proj, pool_w, pool_scale.reshape(1, GROUP_W))


def _diff_kernel(q_ref, k_ref, v_ref, cq_ref, sq_ref, ck_ref, sk_ref, dl_ref, dn_ref, o_ref,
                 k16, v16, *, l_ctx, lam_init):
    t_all = k_ref.shape[0]
    tq = q_ref.shape[0]
    j = pl.program_id(2)
    half = HEAD_DK // 2

    def rope(x, cos, sin):
        return x * cos + _swap_halves(x, half) * sin

    @pl.when(j == 0)
    def _():
        k16[0:l_ctx, :] = k_ref[0:l_ctx, :].astype(BF16)

        def ktile(i, carry):
            r0 = pl.multiple_of(i * tq, tq)
            kt = k_ref[pl.ds(l_ctx + r0, tq), :]
            k16[pl.ds(l_ctx + r0, tq), :] = rope(kt, ck_ref[pl.ds(r0, tq), :],
                                                 sk_ref[pl.ds(r0, tq), :]).astype(BF16)
            return carry

        lax.fori_loop(0, (t_all - l_ctx) // tq, ktile, 0)
        v16[...] = v_ref[...].astype(BF16)

    dl = dl_ref[...]
    lam = (jnp.exp(jnp.sum(dl[0:1, :] * dl[1:2, :], axis=1, keepdims=True))
           - jnp.exp(jnp.sum(dl[2:3, :] * dl[3:4, :], axis=1, keepdims=True)) + lam_init)
    lane = lax.broadcasted_iota(jnp.int32, (1, LANES), 1)

    def attend(q, nk):
        kk = k16[0:nk, :]
        probs = []
        for m in range(2):
            qm = jnp.where((lane < HEAD_DK) == (m == 0), q, 0.0).astype(BF16)
            s = _dot_nt(qm, kk) * HEAD_DK ** -0.5
            e = jnp.exp(s - jnp.max(s, axis=-1, keepdims=True))
            probs.append(e / jnp.sum(e, axis=-1, keepdims=True))
        a = (probs[0] - lam * probs[1]).astype(BF16)
        o = _dot(a, v16[0:nk, :])
        y = o * lax.rsqrt(jnp.mean(o * o, axis=-1, keepdims=True) + EPS) * dn_ref[...]
        o_ref[...] = (y * (1.0 - lam_init)).astype(o_ref.dtype)

    @pl.when(j == 0)
    def _():
        attend(q_ref[...], l_ctx)

    @pl.when(j > 0)
    def _():
        attend(rope(q_ref[...], cq_ref[...], sq_ref[...]), t_all)


def _diff_attention(proj, cos_t, sin_t, diff_lambda, diff_norm, lam_init, n_batch, t_all, l_ctx):
    r = proj.shape[0]
    tq = ROW_TILE
    nq = t_all // tq
    s_len = t_all - l_ctx
    dl = jnp.pad(diff_lambda, ((0, 4), (0, LANES - HEAD_DK)))
    kv = lambda off: pl.BlockSpec((t_all, LANES), lambda b, h, j: (b, off // LANES + h))
    qtab = pl.BlockSpec((tq, LANES), lambda b, h, j: (jnp.maximum(j - 1, 0), 0))
    ktab = pl.BlockSpec((s_len, LANES), lambda b, h, j: (0, 0))
    return pl.pallas_call(
        functools.partial(_diff_kernel, l_ctx=l_ctx, lam_init=lam_init),
        grid=(n_batch, N_HEADS, nq),
        in_specs=[pl.BlockSpec((tq, LANES), lambda b, h, j: (b * nq + j, OFF_DQ // LANES + h)),
                  kv(OFF_DK), kv(OFF_DV), qtab, qtab, ktab, ktab,
                  pl.BlockSpec((8, LANES), lambda b, h, j: (0, 0)),
                  pl.BlockSpec((1, LANES), lambda b, h, j: (0, 0))],
        out_specs=pl.BlockSpec((tq, LANES), lambda b, h, j: (b * nq + j, h)),
        out_shape=jax.ShapeDtypeStruct((r, GROUP_W), BF16),
        scratch_shapes=[pltpu.VMEM((t_all, LANES), BF16), pltpu.VMEM((t_all, LANES), BF16)],
        compiler_params=_cparams(3), name="diff_attn",
    )(proj, proj, proj, cos_t, sin_t, cos_t, sin_t, dl, diff_norm.reshape(1, LANES))


CUM_BLOCK = 256
ONE_BITS_PLUS = 0x3F800001


def _select_segment(aff_ref, r0, t, cap, row_base, idx_ref, gate_ref, sel_ref, csum_ref):
    def bis(_, carry):
        lo, hi = carry
        bits = lax.bitcast_convert_type(aff_ref[r0:r0 + t, :], jnp.int32)
        mid = lo + lax.shift_right_logical(hi - lo, 1)
        cnt = jnp.sum((bits >= mid).astype(jnp.int32), axis=0, keepdims=True)
        ok = cnt >= cap
        return jnp.where(ok, mid, lo), jnp.where(ok, hi, mid)

    lo0 = jnp.zeros((1, LANES), jnp.int32)
    hi0 = jnp.full((1, LANES), ONE_BITS_PLUS, jnp.int32)
    thr, _ = lax.fori_loop(0, 31, bis, (lo0, hi0))
    bits = lax.bitcast_convert_type(aff_ref[r0:r0 + t, :], jnp.int32)
    need = (cap - jnp.sum((bits > thr).astype(jnp.int32), axis=0, keepdims=True)).astype(F32)

    blk = min(CUM_BLOCK, t)
    ii = lax.broadcasted_iota(jnp.int32, (blk, blk), 0)
    jj = lax.broadcasted_iota(jnp.int32, (blk, blk), 1)
    tri16 = jnp.where(ii >= jj, 1.0, 0.0).astype(BF16)
    carry = jnp.zeros((1, LANES), F32)
    for i in range(t // blk):
        b = lax.bitcast_convert_type(aff_ref[r0 + i * blk:r0 + (i + 1) * blk, :], jnp.int32)
        eq = b == thr
        rank = _dot(tri16, jnp.where(eq, 1.0, 0.0).astype(BF16)) + carry
        carry = rank[blk - 1:blk, :]
        sel_ref[i * blk:(i + 1) * blk, :] = jnp.where((b > thr) | (eq & (rank <= need)), 1.0, 0.0)
    carry = jnp.zeros((1, LANES), F32)
    for i in range(t // blk):
        cs = _dot(tri16, sel_ref[i * blk:(i + 1) * blk, :].astype(BF16)) + carry
        carry = cs[blk - 1:blk, :]
        csum_ref[i * blk:(i + 1) * blk, :] = cs

    lane = lax.broadcasted_iota(jnp.int32, (1, LANES), 1)
    slot = (lax.broadcasted_iota(jnp.int32, (1, cap), 1) + 1).astype(F32)
    for e in range(N_EXPERTS):
        def tile(i, res, e=e):
            t0 = pl.multiple_of(i * blk, blk)
            a = aff_ref[pl.ds(r0 + t0, blk), :][:, e:e + 1]
            a_hi = a.astype(BF16).astype(F32)
            a_mid = (a - a_hi).astype(BF16).astype(F32)
            a_lo = a - a_hi - a_mid
            tok = t0 + lax.broadcasted_iota(jnp.int32, (blk, 1), 0)
            tok_lo = (tok & 63).astype(F32)
            tok_hi = lax.shift_right_logical(tok, 6).astype(F32)
            feat = jnp.where(lane == 0, tok_lo,
                   jnp.where(lane == 1, tok_hi,
                   jnp.where(lane == 2, a_hi,
                   jnp.where(lane == 3, a_mid,
                   jnp.where(lane == 4, a_lo, 0.0))))).astype(BF16)
            sel = sel_ref[pl.ds(t0, blk), :][:, e:e + 1]
            cs = csum_ref[pl.ds(t0, blk), :][:, e:e + 1]
            onehot = jnp.where((sel > 0.5) & (cs == slot), 1.0, 0.0).astype(BF16)
            return res + _dot_tn(feat, onehot)

        res = lax.fori_loop(0, t // blk, tile, jnp.zeros((LANES, cap), F32))
        idx = res[0:1, :] + 64.0 * res[1:2, :]
        idx_ref[e:e + 1, :] = idx.astype(jnp.int32) + row_base
        gate_ref[e:e + 1, :] = res[2:3, :] + res[3:4, :] + res[4:5, :]


def _route_kernel(aff_ref, il_ref, gl_ref, ic_ref, gc_ref, sel_ref, csum_ref, *, l_ctx, cap_lat, cap_ctx):
    t_all = aff_ref.shape[0]
    base = pl.program_id(0) * t_all
    _select_segment(aff_ref, 0, l_ctx, cap_ctx, base, ic_ref, gc_ref, sel_ref, csum_ref)
    _select_segment(aff_ref, l_ctx, t_all - l_ctx, cap_lat, base + l_ctx, il_ref, gl_ref, sel_ref, csum_ref)


def _route(aff, n_batch, t_all, l_ctx):
    cap_lat = EC_CAPACITY * (t_all - l_ctx) // N_EXPERTS
    cap_ctx = EC_CAPACITY * l_ctx // N_EXPERTS
    spec = lambda cap: pl.BlockSpec((None, N_EXPERTS, cap), lambda b: (b, 0, 0))
    shp = lambda cap, dt: jax.ShapeDtypeStruct((n_batch, N_EXPERTS, cap), dt)
    return pl.pallas_call(
        functools.partial(_route_kernel, l_ctx=l_ctx, cap_lat=cap_lat, cap_ctx=cap_ctx),
        grid=(n_batch,),
        in_specs=[pl.BlockSpec((t_all, LANES), lambda b: (b, 0))],
        out_specs=[spec(cap_lat), spec(cap_lat), spec(cap_ctx), spec(cap_ctx)],
        out_shape=[shp(cap_lat, jnp.int32), shp(cap_lat, F32), shp(cap_ctx, jnp.int32), shp(cap_ctx, F32)],
        scratch_shapes=[pltpu.VMEM((t_all, LANES), F32), pltpu.VMEM((t_all, LANES), F32)],
        compiler_params=_cparams(1), name="route",
    )(aff)


def _moe_kernel(idx_ref, gate_ref, gt_ref, wg_ref, wu_ref, wd_ref, h_hbm, x_in_hbm, x_hbm,
                hbuf, xbuf, sem_h, sem_x, sem_o):
    del x_in_hbm
    cap = hbuf.shape[0]

    def h_copy(p):
        return pltpu.make_async_copy(h_hbm.at[pl.ds(idx_ref[0, p], 1)], hbuf.at[pl.ds(p, 1)], sem_h)

    def x_copy(p):
        return pltpu.make_async_copy(x_hbm.at[pl.ds(idx_ref[0, p], 1)], xbuf.at[pl.ds(p, 1)], sem_x)

    def o_copy(p):
        return pltpu.make_async_copy(xbuf.at[pl.ds(p, 1)], x_hbm.at[pl.ds(idx_ref[0, p], 1)], sem_o)

    def start_gather(p, carry):
        h_copy(p).start()
        x_copy(p).start()
        return carry

    def wait_h(p, carry):
        h_copy(p).wait()
        return carry

    def wait_x(p, carry):
        x_copy(p).wait()
        return carry

    def start_scatter(p, carry):
        o_copy(p).start()
        return carry

    def wait_scatter(p, carry):
        o_copy(p).wait()
        return carry

    lax.fori_loop(0, cap, start_gather, 0)
    lax.fori_loop(0, cap, wait_h, 0)
    xs = hbuf[...].astype(BF16)
    act =(_silu(_dot(xs, wg_ref[...])) * _dot(xs, wu_ref[...])).astype(BF16)
    y = _dot(act, wd_ref[...]) * gate_ref[...]
    lax.fori_loop(0, cap, wait_x, 0)
    xbuf[...] = xbuf[...] + gt_ref[...] * y
    lax.fori_loop(0, cap, start_scatter, 0)
    lax.fori_loop(0, cap, wait_scatter, 0)


def _moe(x, h, idx, gates, modl, mod_row_of_batch, wg, wu, wd):
    r, d = x.shape
    n_batch, n_exp, cap = idx.shape
    return pl.pallas_call(
        _moe_kernel,
        grid=(n_exp, n_batch),
        in_specs=[pl.BlockSpec((None, 1, cap), lambda e, b: (b * n_exp + e, 0, 0), memory_space=pltpu.SMEM),
                  pl.BlockSpec((None, None, cap, 1), lambda e, b: (b, e, 0, 0)),
                  pl.BlockSpec((None, None, 1, d), lambda e, b: (mod_row_of_batch(b), 5, 0, 0)),
                  pl.BlockSpec((None, d, EXPERT_FF), lambda e, b: (e, 0, 0)),
                  pl.BlockSpec((None, d, EXPERT_FF), lambda e, b: (e, 0, 0)),
                  pl.BlockSpec((None, EXPERT_FF, d), lambda e, b: (e, 0, 0)),
                  pl.BlockSpec(memory_space=pl.ANY),
                  pl.BlockSpec(memory_space=pl.ANY)],
        out_specs=pl.BlockSpec(memory_space=pl.ANY),
        out_shape=jax.ShapeDtypeStruct((r, d), F32),
        scratch_shapes=[pltpu.VMEM((cap, d), F32), pltpu.VMEM((cap, d), F32),
                        pltpu.SemaphoreType.DMA(()), pltpu.SemaphoreType.DMA(()),
                        pltpu.SemaphoreType.DMA(())],
        input_output_aliases={7: 0},
        compiler_params=_cparams(2), name="moe",
    )(idx.reshape(n_batch * n_exp, 1, cap), gates.reshape(n_batch, n_exp, cap, 1), modl, wg, wu, wd, h, x)


def _final_norm_kernel(x_ref, g_ref, o_ref):
    x = x_ref[...]
    o_ref[...] = x * lax.rsqrt(jnp.mean(x * x, axis=-1, keepdims=True) + EPS) * g_ref[...]


def _final_norm(x, g, n_batch, t_all, l_ctx):
    d = x.shape[1]
    tm = ROW_TILE
    tps = t_all // tm
    lat_tiles = (t_all - l_ctx) // tm
    ctx_tiles = l_ctx // tm
    return pl.pallas_call(
        _final_norm_kernel, grid=(n_batch, lat_tiles),
        in_specs=[pl.BlockSpec((tm, d), lambda b, j: (b * tps + ctx_tiles + j, 0)),
                  pl.BlockSpec((1, d), lambda b, j: (0, 0))],
        out_specs=pl.BlockSpec((tm, d), lambda b, j: (b * lat_tiles + j, 0)),
        out_shape=jax.ShapeDtypeStruct((n_batch * (t_all - l_ctx), d), F32),
        compiler_params=_cparams(2), name="final_norm",
    )(x, g.reshape(1, d))


def _rope_tables(pos, dim):
    inv = ROPE_BASE ** (-jnp.arange(0, dim, 2, dtype=F32) / dim)
    ang = pos.astype(F32)[:, None] * inv[None, :]
    return jnp.cos(ang), jnp.sin(ang)


def _retention_tables(t_all):
    cos, sin = _rope_tables(jnp.arange(t_all), HEAD_DK)
    cos_t = jnp.tile(jnp.concatenate([cos, cos], axis=1), (1, LANES // HEAD_DK))
    sin_t = jnp.tile(jnp.concatenate([-sin, sin], axis=1), (1, LANES // HEAD_DK))
    return cos_t, sin_t


def _axial_tables(s_len):
    rows = s_len // GRID_W
    row_pos = jnp.repeat(jnp.arange(rows), GRID_W)
    col_pos = jnp.tile(jnp.arange(GRID_W), rows)
    rc, rs = _rope_tables(row_pos, HEAD_DK // 2)
    cc, cs = _rope_tables(col_pos, HEAD_DK // 2)
    cos = jnp.concatenate([rc, rc, cc, cc], axis=1)
    sin = jnp.concatenate([-rs, rs, -cs, cs], axis=1)
    return jnp.tile(cos, (1, LANES // HEAD_DK)), jnp.tile(sin, (1, LANES // HEAD_DK))


def _pick_tm(t_all):
    tm = t_all // 4
    assert tm % 16 == 0
    return tm


def kernel(x, c, ctx, c_ctx, w_ada, b_ada, norm_mix, norm_ffn, w_in, gdn_conv, gdn_A_log, gdn_dt_bias, gdn_norm, ret_decay_exp, ret_norm, pool_w, pool_scale, diff_lambda, diff_norm, w_out, w_router, w_gate, w_up, w_down, final_norm):
    n_batch, s_len, d = x.shape
    l_ctx = ctx.shape[1]
    depth = w_ada.shape[0]
    t_all = l_ctx + s_len
    assert d == D_MODEL and l_ctx == ROW_TILE and l_ctx == RET_CHUNK and s_len % ROW_TILE == 0
    tm = _pick_tm(t_all)

    xs = jnp.concatenate([ctx, x], axis=1).reshape(n_batch * t_all, d)
    c_all = jnp.concatenate([c_ctx[None, :], c, jnp.zeros((8 - 1 - n_batch, d), F32)], axis=0)
    mod = _ada(c_all, w_ada, b_ada).reshape(depth, 8, ADA_CHUNKS, 1, d)
    ret_cos, ret_sin = _retention_tables(t_all)
    ax_cos, ax_sin = _axial_tables(s_len)

    for i in range(depth):
        lam_init = 0.8 - 0.6 * math.exp(-0.3 * i)
        modl = mod[i]
        w_main = jnp.concatenate([w_in[i][:, :IN_AB0], w_in[i][:, IN_AB1:]], axis=1).astype(BF16)
        w_ab = jnp.pad(w_in[i][:, IN_AB0:IN_AB1], ((0, 0), (0, LANES - (IN_AB1 - IN_AB0)))).astype(BF16)

        h = _normmod(xs, norm_mix[i], modl, 0, t_all)
        proj = _matmul(h, w_main, tm, 1024, "in_proj")
        ab = _matmul(h, w_ab, tm, LANES, "ab_proj")
        col = _gdn_gates(ab, gdn_A_log[i], gdn_dt_bias[i])
        r = col.shape[0]
        colh = jnp.pad(col[:, :80].reshape(r, 5, 2, N_HEADS).transpose(3, 0, 1, 2).reshape(N_HEADS, r, 10),
                       ((0, 0), (0, 0), (0, 6)))
        rowh = col[:, :16].reshape(r // GDN_CHUNK, GDN_CHUNK, 2, N_HEADS).transpose(3, 0, 2, 1)
        conv_w = jnp.pad(gdn_conv[i], ((0, 8 - GDN_CONV_K), (0, 0)))

        y_gdn = _gdn(proj, colh, rowh, conv_w, gdn_norm[i], n_batch, t_all, l_ctx)
        y_ret = _retention(proj, ret_cos, ret_sin, ret_decay_exp[i], ret_norm[i], n_batch, t_all, l_ctx)
        y_pool = _pool(proj, pool_w[i], pool_scale[i], n_batch, t_all, l_ctx)
        y_diff = _diff_attention(proj, ax_cos, ax_sin, diff_lambda[i], diff_norm[i], lam_init,
                                 n_batch, t_all, l_ctx)
        y = jnp.concatenate([y_gdn, y_ret, y_pool, y_diff], axis=1)
        xs = _out_proj(y, w_out[i].astype(BF16), xs, modl, 2, t_all, l_ctx, tm, 512)

        h2, aff = _normmod(xs, norm_ffn[i], modl, 3, t_all, w_router=w_router[i])
        idx_lat, gate_lat, idx_ctx, gate_ctx = _route(aff, n_batch, t_all, l_ctx)
        wg = w_gate[i].astype(BF16)
        wu = w_up[i].astype(BF16)
        wd = w_down[i].astype(BF16)
        xs = _moe(xs, h2, idx_lat, gate_lat, modl, lambda b: 1 + b, wg, wu, wd)
        if i < depth - 1:
            xs = _moe(xs, h2, idx_ctx, gate_ctx, modl, lambda b: 0 * b, wg, wu, wd)
    return _final_norm(xs, final_norm, n_batch, t_all, l_ctx).reshape(n_batch, s_len, d)
```

```python
import functools
import math

import jax
import jax.numpy as jnp
from jax import lax
from jax.experimental import pallas as pl
from jax.experimental.pallas import tpu as pltpu

F32 = jnp.float32
BF16 = jnp.bfloat16
HIGHEST = lax.Precision.HIGHEST

D_MODEL = 4096
GROUP_W = 1024
N_HEADS = 8
HEAD_DV = 128
HEAD_DK = 64
ADA_CHUNKS = 6
EPS = 1e-6
ROPE_BASE = 10000.0
GDN_CONV_K = 5
GDN_CHUNK = 64
RET_CHUNK = 256
POOL_WINDOWS = (2, 4, 8, 16)
POOL_GROUP = 256
N_EXPERTS = 16
EXPERT_FF = 256
EC_CAPACITY = 2
GRID_W = 64
IN_AB0, IN_AB1 = 4096, 4128
MAIN_W = 11264
OFF_GQ, OFF_GK, OFF_GV, OFF_GZ = 0, 1024, 2048, 3072
OFF_RQ, OFF_RK, OFF_RV, OFF_RG = 4096, 4608, 5120, 6144
OFF_POOL = 7168
OFF_DQ, OFF_DK, OFF_DV = 8192, 9216, 10240

LANES = 128
ROW_TILE = 256
VMEM_LIMIT_BYTES = 58 * 1024 * 1024


def _cparams(n_axes):
    return pltpu.CompilerParams(dimension_semantics=("arbitrary",) * n_axes,
                                vmem_limit_bytes=VMEM_LIMIT_BYTES)


def _silu(x):
    return x / (1.0 + jnp.exp(-x))


def _sigmoid(x):
    return 1.0 / (1.0 + jnp.exp(-x))


def _dot(a, b):
    return jnp.dot(a, b, preferred_element_type=F32)


def _dot_nt(a, b):
    return lax.dot_general(a, b, (((1,), (1,)), ((), ())), preferred_element_type=F32)


def _dot_tn(a, b):
    return lax.dot_general(a, b, (((0,), (0,)), ((), ())), preferred_element_type=F32)


def _dot_hi(a, b):
    return jnp.dot(a, b, precision=HIGHEST, preferred_element_type=F32)


def _swap_halves(x, group):
    half = group // 2
    lane = lax.broadcasted_iota(jnp.int32, x.shape, x.ndim - 1)
    return jnp.where((lane % group) < half,
                     pltpu.roll(x, LANES - half, axis=x.ndim - 1),
                     pltpu.roll(x, half, axis=x.ndim - 1))


def _ada_kernel(c_ref, w_ref, b_ref, o_ref):
    s = _silu(c_ref[...]).astype(BF16)
    o_ref[...] = _dot(s, w_ref[...].astype(BF16)) + b_ref[...]


def _ada(c_all, w_ada, b_ada):
    depth, d, n = w_ada.shape
    tn = 512
    return pl.pallas_call(
        _ada_kernel,
        grid=(depth, n // tn),
        in_specs=[pl.BlockSpec((8, d), lambda l, j: (0, 0)),
                  pl.BlockSpec((None, d, tn), lambda l, j: (l, 0, j)),
                  pl.BlockSpec((None, 1, tn), lambda l, j: (l, 0, j))],
        out_specs=pl.BlockSpec((None, 8, tn), lambda l, j: (l, 0, j)),
        out_shape=jax.ShapeDtypeStruct((depth, 8, n), F32),
        compiler_params=_cparams(2),
        name="ada",
    )(c_all, w_ada, b_ada.reshape(depth, 1, n))


def _mod_row(i, tiles_per_sample):
    return jnp.where(i % tiles_per_sample == 0, 0, 1 + i // tiles_per_sample)


def _normmod_value(x_ref, g_ref, sh_ref, sc_ref):
    x = x_ref[...]
    ms = jnp.mean(x * x, axis=-1, keepdims=True)
    y = x * lax.rsqrt(ms + EPS) * g_ref[...]
    return y * (1.0 + sc_ref[...]) + sh_ref[...]


def _normmod_kernel(x_ref, g_ref, sh_ref, sc_ref, h_ref):
    h_ref[...] = _normmod_value(x_ref, g_ref, sh_ref, sc_ref).astype(h_ref.dtype)


def _normmod_router_kernel(x_ref, g_ref, sh_ref, sc_ref, wr_ref, h_ref, aff_ref):
    h = _normmod_value(x_ref, g_ref, sh_ref, sc_ref)
    h_ref[...] = h.astype(h_ref.dtype)
    logits = _dot_hi(h, wr_ref[...])
    lane = lax.broadcasted_iota(jnp.int32, logits.shape, 1)
    logits = jnp.where(lane < N_EXPERTS, logits, -jnp.inf)
    e = jnp.exp(logits - jnp.max(logits, axis=-1, keepdims=True))
    aff_ref[...] = e / jnp.sum(e, axis=-1, keepdims=True)


def _normmod(x, g, modl, shift_chunk, t_all, w_router=None):
    r, d = x.shape
    tm = ROW_TILE
    tps = t_all // tm
    mod_spec = lambda chunk: pl.BlockSpec(
        (None, None, 1, d), lambda i: (_mod_row(i, tps), chunk, 0, 0))
    in_specs = [pl.BlockSpec((tm, d), lambda i: (i, 0)),
                pl.BlockSpec((1, d), lambda i: (0, 0)),
                mod_spec(shift_chunk), mod_spec(shift_chunk + 1)]
    h_spec = pl.BlockSpec((tm, d), lambda i: (i, 0))
    h_shape = jax.ShapeDtypeStruct((r, d), BF16 if w_router is None else F32)
    if w_router is None:
        return pl.pallas_call(
            _normmod_kernel, grid=(r // tm,), in_specs=in_specs, out_specs=h_spec,
            out_shape=h_shape, compiler_params=_cparams(1), name="normmod",
        )(x, g.reshape(1, d), modl, modl)
    wr = jnp.pad(w_router, ((0, 0), (0, LANES - N_EXPERTS)))
    return pl.pallas_call(
        _normmod_router_kernel, grid=(r // tm,),
        in_specs=in_specs + [pl.BlockSpec((d, LANES), lambda i: (0, 0))],
        out_specs=[h_spec, pl.BlockSpec((tm, LANES), lambda i: (i, 0))],
        out_shape=[h_shape, jax.ShapeDtypeStruct((r, LANES), F32)],
        compiler_params=_cparams(1), name="normmod_router",
    )(x, g.reshape(1, d), modl, modl, wr)


def _mm_kernel(x_ref, w_ref, o_ref):
    o_ref[...] = _dot(x_ref[...], w_ref[...]).astype(o_ref.dtype)


def _matmul(x, w, tm, tn, name):
    m, k = x.shape
    n = w.shape[1]
    return pl.pallas_call(
        _mm_kernel, grid=(m // tm, n // tn),
        in_specs=[pl.BlockSpec((tm, k), lambda i, j: (i, 0)),
                  pl.BlockSpec((k, tn), lambda i, j: (0, j))],
        out_specs=pl.BlockSpec((tm, tn), lambda i, j: (i, j)),
        out_shape=jax.ShapeDtypeStruct((m, n), F32),
        compiler_params=_cparams(2), name=name,
    )(x, w)


def _mm_res_kernel(y_ref, w_ref, x_ref, gs_ref, gc_ref, o_ref, *, tm, tiles_per_sample, l_ctx):
    acc = _dot(y_ref[...], w_ref[...])
    i = pl.program_id(0)
    row = (i % tiles_per_sample) * tm + lax.broadcasted_iota(jnp.int32, (tm, 1), 0)
    gate = jnp.where(row < l_ctx, gc_ref[...], gs_ref[...])
    o_ref[...] = x_ref[...] + gate * acc


def _out_proj(y, w, x, modl, gate_chunk, t_all, l_ctx, tm, tn):
    m, k = y.shape
    n = w.shape[1]
    tps = t_all // tm
    kern = functools.partial(_mm_res_kernel, tm=tm, tiles_per_sample=tps, l_ctx=l_ctx)
    return pl.pallas_call(
        kern, grid=(m // tm, n // tn),
        in_specs=[pl.BlockSpec((tm, k), lambda i, j: (i, 0)),
                  pl.BlockSpec((k, tn), lambda i, j: (0, j)),
                  pl.BlockSpec((tm, tn), lambda i, j: (i, j)),
                  pl.BlockSpec((None, None, 1, tn), lambda i, j: (1 + i // tps, gate_chunk, 0, j)),
                  pl.BlockSpec((None, None, 1, tn), lambda i, j: (0, gate_chunk, 0, j))],
        out_specs=pl.BlockSpec((tm, tn), lambda i, j: (i, j)),
        out_shape=jax.ShapeDtypeStruct((m, n), F32),
        compiler_params=_cparams(2), name="out_proj",
    )(y, w, x, modl, modl)


def _gdn_gate_kernel(ab_ref, alog_ref, dtb_ref, o_ref):
    ab = ab_ref[...]
    lane = lax.broadcasted_iota(jnp.int32, (1, LANES), 1)
    z = ab + dtb_ref[...]
    softplus = jnp.maximum(z, 0.0) + jnp.log1p(jnp.exp(-jnp.abs(z)))
    g = -jnp.exp(alog_ref[...]) * softplus
    beta = _sigmoid(ab)
    c = GDN_CHUNK
    ii = lax.broadcasted_iota(jnp.int32, (c, c), 0)
    jj = lax.broadcasted_iota(jnp.int32, (c, c), 1)
    ltri = jnp.where(ii >= jj, 1.0, 0.0).astype(F32)
    for n in range(ab.shape[0] // c):
        gk = g[n * c:(n + 1) * c]
        pre = _dot_hi(ltri, gk)
        tot = pre[c - 1:c, :]
        suf = tot - pre + gk
        gc = jnp.where(lane < N_HEADS, pre, suf)
        e1 = jnp.exp(gc)
        e2 = jnp.exp(tot - gc)
        gt = jnp.broadcast_to(jnp.exp(tot), gc.shape)
        out = jnp.where(lane < 16, gc,
              jnp.where(lane < 32, beta[n * c:(n + 1) * c],
              jnp.where(lane < 48, pltpu.roll(e1, 32, axis=1),
              jnp.where(lane < 64, pltpu.roll(e2, 48, axis=1),
              jnp.where(lane < 80, pltpu.roll(gt, 64, axis=1), 0.0)))))
        o_ref[n * c:(n + 1) * c, :] = out


def _gdn_gates(ab, a_log, dt_bias):
    r = ab.shape[0]
    tm = ROW_TILE
    pad = lambda p: jnp.pad(p.reshape(1, 2 * N_HEADS), ((0, 0), (0, LANES - 2 * N_HEADS)))
    return pl.pallas_call(
        _gdn_gate_kernel, grid=(r // tm,),
        in_specs=[pl.BlockSpec((tm, LANES), lambda i: (i, 0)),
                  pl.BlockSpec((1, LANES), lambda i: (0, 0)),
                  pl.BlockSpec((1, LANES), lambda i: (0, 0))],
        out_specs=pl.BlockSpec((tm, LANES), lambda i: (i, 0)),
        out_shape=jax.ShapeDtypeStruct((r, LANES), F32),
        compiler_params=_cparams(1), name="gdn_gates",
    )(ab, pad(a_log), pad(dt_bias))


CONV_PAD = 8
GDN_GROUP = 4


def _gdn_kernel(q_ref, k_ref, v_ref, z_ref, wq_ref, wk_ref, wv_ref, col_ref, row_ref, gn_ref,
                o_ref, qn, kn, vn, pad, oacc, u_s, wq_s, ak_s, *, l_ctx):
    t_all = q_ref.shape[0]
    c = GDN_CHUNK
    n_chunks = t_all // c
    nc_ctx = l_ctx // c
    rt = ROW_TILE

    def conv_into(src_ref, w_ref, dst, l2, scale):
        w = w_ref[...]
        for (r0, n) in ((0, l_ctx), (l_ctx, t_all - l_ctx)):
            pad[0:CONV_PAD, :] = jnp.zeros((CONV_PAD, LANES), F32)
            pad[CONV_PAD:CONV_PAD + n, :] = src_ref[r0:r0 + n, :]
            pad[CONV_PAD + n:2 * CONV_PAD + n, :] = jnp.zeros((CONV_PAD, LANES), F32)
            for t0 in range(0, n, rt):
                base = CONV_PAD - (GDN_CONV_K - 1) // 2 + t0
                acc = w[0:1, :] * pad[base:base + rt, :]
                for j in range(1, GDN_CONV_K):
                    acc = acc + w[j:j + 1, :] * pad[base + j:base + j + rt, :]
                s = _silu(acc)
                if l2:
                    s = s * lax.rsqrt(jnp.sum(s * s, axis=-1, keepdims=True) + EPS) * scale
                dst[r0 + t0:r0 + t0 + rt, :] = s

    conv_into(q_ref, wq_ref, qn, True, HEAD_DV ** -0.5)
    conv_into(k_ref, wk_ref, kn, True, 1.0)
    conv_into(v_ref, wv_ref, vn, False, 1.0)
    oacc[...] = jnp.zeros(oacc.shape, F32)

    gr = GDN_GROUP * c
    ii = lax.broadcasted_iota(jnp.int32, (gr, gr), 0)
    jj = lax.broadcasted_iota(jnp.int32, (gr, gr), 1)
    same = (ii // c) == (jj // c)
    eye = jnp.where(ii == jj, 1.0, 0.0).astype(F32)

    def prep_body(gi, carry):
        r0 = pl.multiple_of(gi * gr, gr)
        q = qn[pl.ds(r0, gr), :]
        k = kn[pl.ds(r0, gr), :]
        v = vn[pl.ds(r0, gr), :]
        col = col_ref[pl.ds(r0, gr), :]
        k16 = k.astype(BF16)
        kq = _dot_nt(jnp.concatenate([k16, q.astype(BF16)], axis=0), k16)
        res = []
        for d in range(2):
            gc = col[:, d:d + 1]
            beta = col[:, 2 + d:3 + d]
            e1 = col[:, 4 + d:5 + d]
            e2 = col[:, 6 + d:7 + d]
            gcr = row_ref[gi, d:d + 1, :]
            incl = same & ((ii >= jj) if d == 0 else (ii <= jj))
            strict = same & ((ii > jj) if d == 0 else (ii < jj))
            dm = jnp.exp(jnp.where(incl, gc - gcr, -jnp.inf))
            p = -jnp.where(strict, beta * kq[0:gr] * dm, 0.0)
            t = eye + p
            p16 = p.astype(BF16)
            for _ in range(5):
                p16 = _dot(p16, p16).astype(BF16)
                t = t + _dot(t.astype(BF16), p16)
            rhs = jnp.concatenate([(v * beta).astype(BF16), (k * (beta * e1)).astype(BF16)], axis=1)
            uw = _dot(t.astype(BF16), rhs)
            res.append((uw[:, 0:HEAD_DV], uw[:, HEAD_DV:2 * HEAD_DV].astype(BF16), (q * e1).astype(BF16),
                        (kq[gr:2 * gr] * dm).astype(BF16), (k * e2).T.astype(BF16)))
        for d, (u, w16, qd16, att16, kdt16) in enumerate(res):
            u_s[d, pl.ds(r0, gr), :] = u
            for n in range(GDN_GROUP):
                rows = slice(n * c, (n + 1) * c)
                wq_s[d, pl.ds(2 * r0 + 2 * n * c, c), :] = w16[rows]
                wq_s[d, pl.ds(2 * r0 + 2 * n * c + c, c), :] = qd16[rows]
                ak_s[d, pl.ds(3 * r0 + 3 * n * c, c), :] = att16[rows, rows]
                ak_s[d, pl.ds(3 * r0 + 3 * n * c + c, 2 * c), :] = kdt16[:, rows]
        return carry

    lax.fori_loop(0, t_all // gr, prep_body, 0)

    def chunk(d, ci, state):
        r0 = pl.multiple_of(ci * c, c)
        s16 = state.astype(BF16)
        ws = _dot(wq_s[d, pl.ds(2 * r0, 2 * c), :], s16)
        vn16 = (u_s[d, pl.ds(r0, c), :] - ws[0:c]).astype(BF16)
        av = _dot(ak_s[d, pl.ds(3 * r0, 3 * c), :], vn16)
        gt = col_ref[pl.ds(r0, 1), :][:, 8 + d:9 + d]
        oacc[pl.ds(r0, c), :] += ws[c:2 * c] + av[0:c]
        return state * gt + av[c:3 * c]

    def body(i, carry):
        sf, sb = carry
        cb = jnp.where(i < nc_ctx, nc_ctx - 1 - i, n_chunks - 1 - i + nc_ctx)
        sf = chunk(0, i, sf)
        sb = chunk(1, cb, sb)
        return sf, sb

    zero = jnp.zeros((HEAD_DV, HEAD_DV), F32)
    lax.fori_loop(0, n_chunks, body, (zero, zero))

    def post(i, carry):
        r0 = pl.multiple_of(i * rt, rt)
        o = oacc[pl.ds(r0, rt), :]
        y = o * lax.rsqrt(jnp.mean(o * o, axis=-1, keepdims=True) + EPS) * gn_ref[...]
        o_ref[pl.ds(r0, rt), :] = (y * _silu(z_ref[pl.ds(r0, rt), :])).astype(o_ref.dtype)
        return carry

    lax.fori_loop(0, t_all // rt, post, 0)


def _gdn(proj, colh, rowh, conv_w, gdn_norm, n_batch, t_all, l_ctx):
    r = proj.shape[0]
    cb = lambda off: (lambda b, h: (b, off // LANES + h))
    wb = lambda off: (lambda b, h: (0, off // LANES + h))
    seq = lambda im: pl.BlockSpec((t_all, LANES), im)
    ngr = t_all // (GDN_GROUP * GDN_CHUNK)
    return pl.pallas_call(
        functools.partial(_gdn_kernel, l_ctx=l_ctx),
        grid=(n_batch, N_HEADS),
        in_specs=[seq(cb(OFF_GQ)), seq(cb(OFF_GK)), seq(cb(OFF_GV)), seq(cb(OFF_GZ)),
                  pl.BlockSpec((8, LANES), wb(0)), pl.BlockSpec((8, LANES), wb(GROUP_W)),
                  pl.BlockSpec((8, LANES), wb(2 * GROUP_W)),
                  pl.BlockSpec((None, t_all, 16), lambda b, h: (h, b, 0)),
                  pl.BlockSpec((None, ngr, 2, GDN_GROUP * GDN_CHUNK), lambda b, h: (h, b, 0, 0)),
                  pl.BlockSpec((1, LANES), lambda b, h: (0, 0))],
        out_specs=pl.BlockSpec((t_all, LANES), lambda b, h: (b, h)),
        out_shape=jax.ShapeDtypeStruct((r, GROUP_W), BF16),
        scratch_shapes=[pltpu.VMEM((t_all, LANES), F32), pltpu.VMEM((t_all, LANES), F32),
                        pltpu.VMEM((t_all, LANES), F32),
                        pltpu.VMEM((t_all + 2 * CONV_PAD, LANES), F32),
                        pltpu.VMEM((t_all, LANES), F32),
                        pltpu.VMEM((2, t_all, LANES), F32),
                        pltpu.VMEM((2, 2 * t_all, LANES), BF16),
                        pltpu.VMEM((2, 3 * t_all, GDN_CHUNK), BF16)],
        compiler_params=_cparams(2), name="gdn",
    )(proj, proj, proj, proj, conv_w, conv_w, conv_w, colh, rowh, gdn_norm.reshape(1, LANES))


def _ret_kernel(q_ref, k_ref, v_ref, g_ref, cos_ref, sin_ref, rde_ref, rn_ref, o_ref,
                oacc, dmat, qdec, kdec, gtm, *, l_ctx):
    t_all = q_ref.shape[0]
    c = RET_CHUNK
    n_chunks = t_all // c
    pair = pl.program_id(1)
    lane = lax.broadcasted_iota(jnp.int32, (1, LANES), 1)
    lgv = jnp.log1p(-jnp.exp2(-rde_ref[...]))

    ii = lax.broadcasted_iota(jnp.int32, (c, c), 0)
    jj = lax.broadcasted_iota(jnp.int32, (c, c), 1)
    pos = lax.broadcasted_iota(jnp.int32, (c, LANES), 0).astype(F32)
    lane_c = lax.broadcasted_iota(jnp.int32, (c, LANES), 1)
    srow = lax.broadcasted_iota(jnp.int32, (LANES, 2 * HEAD_DV), 0)
    scol = lax.broadcasted_iota(jnp.int32, (LANES, 2 * HEAD_DV), 1)
    for d in range(2):
        lg = [jnp.sum(jnp.where(lane == 2 * pair + a, lgv[d:d + 1, :], 0.0), axis=1, keepdims=True)
              for a in range(2)]
        for a in range(2):
            dist = (ii - jj) if d == 0 else (jj - ii)
            dmat[a * 2 + d] = jnp.exp(jnp.where(dist >= 0, dist.astype(F32) * lg[a], -jnp.inf))
        lg_l = jnp.where(lane_c < HEAD_DK, lg[0], lg[1])
        qpow = (pos + 1.0) if d == 0 else (c - pos)
        kpow = (c - 1.0 - pos) if d == 0 else pos
        qdec[d] = jnp.exp(qpow * lg_l)
        kdec[d] = jnp.exp(kpow * lg_l)
        blk0 = (srow < HEAD_DK) & (scol < HEAD_DV)
        blk1 = (srow >= HEAD_DK) & (scol >= HEAD_DV)
        gtm[d] = jnp.where(blk0, jnp.exp(c * lg[0]), jnp.where(blk1, jnp.exp(c * lg[1]), 0.0))
    blockmask = jnp.where(((srow < HEAD_DK) & (scol < HEAD_DV)) | ((srow >= HEAD_DK) & (scol >= HEAD_DV)),
                          1.0, 0.0).astype(F32)
    oacc[...] = jnp.zeros(oacc.shape, F32)

    def chunk(d, ci, state):
        r0 = pl.multiple_of(ci * c, c)
        cos = cos_ref[pl.ds(r0, c), :]
        sin = sin_ref[pl.ds(r0, c), :]
        q = q_ref[pl.ds(r0, c), :]
        k = k_ref[pl.ds(r0, c), :]
        q = q * cos + _swap_halves(q, HEAD_DK) * sin
        k = (k * cos + _swap_halves(k, HEAD_DK) * sin) * HEAD_DK ** -0.5
        v16 = v_ref[pl.ds(r0, c), :].astype(BF16)
        k16 = k.astype(BF16)
        parts = []
        for a in range(2):
            qa = jnp.where((lane_c < HEAD_DK) == (a == 0), q, 0.0).astype(BF16)
            att = (_dot_nt(qa, k16) * dmat[a * 2 + d]).astype(BF16)
            parts.append(_dot(att, v16[:, a * HEAD_DV:(a + 1) * HEAD_DV]))
        s16 = state.astype(BF16)
        o = jnp.concatenate(parts, axis=1) + _dot((q * qdec[d]).astype(BF16), s16)
        new_state = state * gtm[d] + blockmask * _dot_tn((k * kdec[d]).astype(BF16), v16)
        oacc[pl.ds(r0, c), :] += o
        return new_state

    nc_ctx = l_ctx // c

    def body(i, carry):
        sf, sb = carry
        cb = jnp.where(i < nc_ctx, nc_ctx - 1 - i, n_chunks - 1 - i + nc_ctx)
        sf = chunk(0, i, sf)
        sb = chunk(1, cb, sb)
        return sf, sb

    zero = jnp.zeros((LANES, 2 * HEAD_DV), F32)
    lax.fori_loop(0, n_chunks, body, (zero, zero))

    def post(i, carry):
        r0 = pl.multiple_of(i * c, c)
        o = oacc[pl.ds(r0, c), :]
        gate = g_ref[pl.ds(r0, c), :]
        ys = []
        for a in range(2):
            oa = o[:, a * HEAD_DV:(a + 1) * HEAD_DV]
            mu = jnp.mean(oa, axis=-1, keepdims=True)
            var = jnp.mean(jnp.square(oa - mu), axis=-1, keepdims=True)
            ys.append((oa - mu) * lax.rsqrt(var + EPS) * rn_ref[:, a * HEAD_DV:(a + 1) * HEAD_DV])
        y = jnp.concatenate(ys, axis=1) * _silu(gate)
        o_ref[pl.ds(r0, c), :] = y.astype(o_ref.dtype)
        return carry

    lax.fori_loop(0, n_chunks, post, 0)


def _retention(proj, cos_t, sin_t, ret_decay_exp, ret_norm, n_batch, t_all, l_ctx):
    r = proj.shape[0]
    c = RET_CHUNK
    rde = jnp.pad(ret_decay_exp, ((0, 6), (0, LANES - N_HEADS)), constant_values=8.0)
    w2 = 2 * HEAD_DV
    return pl.pallas_call(
        functools.partial(_ret_kernel, l_ctx=l_ctx),
        grid=(n_batch, N_HEADS // 2),
        in_specs=[pl.BlockSpec((t_all, LANES), lambda b, p: (b, OFF_RQ // LANES + p)),
                  pl.BlockSpec((t_all, LANES), lambda b, p: (b, OFF_RK // LANES + p)),
                  pl.BlockSpec((t_all, w2), lambda b, p: (b, OFF_RV // w2 + p)),
                  pl.BlockSpec((t_all, w2), lambda b, p: (b, OFF_RG // w2 + p)),
                  pl.BlockSpec((t_all, LANES), lambda b, p: (0, 0)),
                  pl.BlockSpec((t_all, LANES), lambda b, p: (0, 0)),
                  pl.BlockSpec((8, LANES), lambda b, p: (0, 0)),
                  pl.BlockSpec((1, w2), lambda b, p: (0, p))],
        out_specs=pl.BlockSpec((t_all, w2), lambda b, p: (b, p)),
        out_shape=jax.ShapeDtypeStruct((r, GROUP_W), BF16),
        scratch_shapes=[pltpu.VMEM((t_all, w2), F32),
                        pltpu.VMEM((4, c, c), F32),
                        pltpu.VMEM((2, c, LANES), F32), pltpu.VMEM((2, c, LANES), F32),
                        pltpu.VMEM((2, LANES, w2), F32)],
        compiler_params=_cparams(2), name="retention",
    )(proj, proj, proj, proj, cos_t, sin_t, rde, ret_norm.reshape(1, GROUP_W))


POOL_PAD = 8


def _pool_kernel(x_ref, w_ref, sc_ref, o_ref, pad, *, l_ctx):
    t_all = x_ref.shape[0]
    rt = ROW_TILE
    grp = pl.program_id(1)
    w16 = w_ref[...].astype(BF16)
    for gi, win in enumerate(POOL_WINDOWS):
        @pl.when(grp == gi)
        def _(win=win):
            half = win // 2
            for (r0, n) in ((0, l_ctx), (l_ctx, t_all - l_ctx)):
                pad[0:POOL_PAD, :] = jnp.zeros((POOL_PAD, POOL_GROUP), F32)
                pad[POOL_PAD:POOL_PAD + n, :] = x_ref[r0:r0 + n, :]
                pad[POOL_PAD + n:2 * POOL_PAD + n, :] = jnp.zeros((POOL_PAD, POOL_GROUP), F32)

                for t0 in range(0, n, rt):
                    acc = pad[t0 + POOL_PAD - half:t0 + POOL_PAD - half + rt, :]
                    for j in range(1, win):
                        acc = acc + pad[t0 + POOL_PAD - half + j:t0 + POOL_PAD - half + j + rt, :]
                    t = t0 + lax.broadcasted_iota(jnp.int32, (rt, 1), 0)
                    cnt = jnp.minimum(t + half, n) - jnp.maximum(t - half, 0)
                    x = pad[t0 + POOL_PAD:t0 + POOL_PAD + rt, :]
                    dlt = (acc / cnt.astype(F32) - x).astype(BF16)
                    o_ref[r0 + t0:r0 + t0 + rt, :] = (_dot(dlt, w16) * sc_ref[...]).astype(o_ref.dtype)


def _pool(proj, pool_w, pool_scale, n_batch, t_all, l_ctx):
    r = proj.shape[0]
    g = POOL_GROUP
    return pl.pallas_call(
        functools.partial(_pool_kernel, l_ctx=l_ctx),
        grid=(n_batch, len(POOL_WINDOWS)),
        in_specs=[pl.BlockSpec((t_all, g), lambda b, i: (b, OFF_POOL // g + i)),
                  pl.BlockSpec((None, g, g), lambda b, i: (i, 0, 0)),
                  pl.BlockSpec((1, g), lambda b, i: (0, i))],
        out_specs=pl.BlockSpec((t_all, g), lambda b, i: (b, i)),
        out_shape=jax.ShapeDtypeStruct((r, GROUP_W), BF16),
        scratch_shapes=[pltpu.VMEM((t_all + 2 * POOL_PAD, g), F32)],
        compiler_params=_cparams(2), name="pool",
    )(proj, pool_w, pool_scale.reshape(1, GROUP_W))


DIFF_TQ = 512


def _diff_kernel(q_ref, k_ref, v_ref, cos_ref, sin_ref, dl_ref, dn_ref, o_ref, k16, v16,
                 *, l_ctx, lam_init):
    t_all = k_ref.shape[0]
    tq = DIFF_TQ
    j = pl.program_id(2)
    half = HEAD_DK // 2

    def rope(x, r0, n):
        return x * cos_ref[pl.ds(r0, n), :] + _swap_halves(x, half) * sin_ref[pl.ds(r0, n), :]

    @pl.when(j == 0)
    def _():
        k16[0:l_ctx, :] = k_ref[0:l_ctx, :].astype(BF16)

        def ktile(i, carry):
            r0 = pl.multiple_of(i * tq, tq)
            k16[pl.ds(l_ctx + r0, tq), :] = rope(k_ref[pl.ds(l_ctx + r0, tq), :], r0, tq).astype(BF16)
            return carry

        lax.fori_loop(0, (t_all - l_ctx) // tq, ktile, 0)
        v16[...] = v_ref[...].astype(BF16)

    dl = dl_ref[...]
    lam = (jnp.exp(jnp.sum(dl[0:1, :] * dl[1:2, :], axis=1, keepdims=True))
           - jnp.exp(jnp.sum(dl[2:3, :] * dl[3:4, :], axis=1, keepdims=True)) + lam_init)
    lane = lax.broadcasted_iota(jnp.int32, (1, LANES), 1)

    def attend(q, nk, row0):
        kk = k16[0:nk, :]
        vv = v16[0:nk, :]
        qs = q * HEAD_DK ** -0.5
        outs = []
        for m in range(2):
            qm = jnp.where((lane < HEAD_DK) == (m == 0), qs, 0.0).astype(BF16)
            s = _dot_nt(qm, kk)
            e = jnp.exp(s - jnp.max(s, axis=-1, keepdims=True))
            den = jnp.sum(e, axis=-1, keepdims=True)
            outs.append(_dot(e.astype(BF16), vv) / den)
        o = outs[0] - lam * outs[1]
        y = o * lax.rsqrt(jnp.mean(o * o, axis=-1, keepdims=True) + EPS) * dn_ref[...]
        o_ref[pl.ds(row0, q.shape[0]), :] = (y * (1.0 - lam_init)).astype(o_ref.dtype)

    @pl.when(j == 0)
    def _():
        attend(q_ref[0:l_ctx, :], l_ctx, 0)

    @pl.when(j > 0)
    def _():
        r0 = pl.multiple_of((j - 1) * tq, tq)
        attend(rope(q_ref[pl.ds(l_ctx + r0, tq), :], r0, tq), t_all, pl.multiple_of(l_ctx + r0, ROW_TILE))


def _diff_attention(proj, cos_t, sin_t, diff_lambda, diff_norm, lam_init, n_batch, t_all, l_ctx):
    r = proj.shape[0]
    s_len = t_all - l_ctx
    nj = 1 + s_len // DIFF_TQ
    dl = jnp.pad(diff_lambda, ((0, 4), (0, LANES - HEAD_DK)))
    seq = lambda off: pl.BlockSpec((t_all, LANES), lambda b, h, j: (b, off // LANES + h))
    tab = pl.BlockSpec((s_len, LANES), lambda b, h, j: (0, 0))
    return pl.pallas_call(
        functools.partial(_diff_kernel, l_ctx=l_ctx, lam_init=lam_init),
        grid=(n_batch, N_HEADS, nj),
        in_specs=[seq(OFF_DQ), seq(OFF_DK), seq(OFF_DV), tab, tab,
                  pl.BlockSpec((8, LANES), lambda b, h, j: (0, 0)),
                  pl.BlockSpec((1, LANES), lambda b, h, j: (0, 0))],
        out_specs=pl.BlockSpec((t_all, LANES), lambda b, h, j: (b, h)),
        out_shape=jax.ShapeDtypeStruct((r, GROUP_W), BF16),
        scratch_shapes=[pltpu.VMEM((t_all, LANES), BF16), pltpu.VMEM((t_all, LANES), BF16)],
        compiler_params=_cparams(3), name="diff_attn",
    )(proj, proj, proj, cos_t, sin_t, dl, diff_norm.reshape(1, LANES))


CUM_BLOCK = 256
ROUTE_TILE = 1024
ONE_BITS_PLUS = 0x3F800001


def _select_segment(aff_ref, r0, t, cap, row_base, idx_ref, gate_ref, sel_ref, csum_ref):
    def bis(_, carry):
        lo, hi = carry
        bits = lax.bitcast_convert_type(aff_ref[r0:r0 + t, :], jnp.int32)
        mid = lo + lax.shift_right_logical(hi - lo, 1)
        cnt = jnp.sum((bits >= mid).astype(jnp.int32), axis=0, keepdims=True)
        ok = cnt >= cap
        return jnp.where(ok, mid, lo), jnp.where(ok, hi, mid)

    lo0 = jnp.zeros((1, LANES), jnp.int32)
    hi0 = jnp.full((1, LANES), ONE_BITS_PLUS, jnp.int32)
    thr, _ = lax.fori_loop(0, 31, bis, (lo0, hi0))
    bits = lax.bitcast_convert_type(aff_ref[r0:r0 + t, :], jnp.int32)
    need = (cap - jnp.sum((bits > thr).astype(jnp.int32), axis=0, keepdims=True)).astype(F32)

    blk = min(CUM_BLOCK, t)
    ii = lax.broadcasted_iota(jnp.int32, (blk, blk), 0)
    jj = lax.broadcasted_iota(jnp.int32, (blk, blk), 1)
    tri16 = jnp.where(ii >= jj, 1.0, 0.0).astype(BF16)
    carry = jnp.zeros((1, LANES), F32)
    for i in range(t // blk):
        b = lax.bitcast_convert_type(aff_ref[r0 + i * blk:r0 + (i + 1) * blk, :], jnp.int32)
        eq = b == thr
        rank = _dot(tri16, jnp.where(eq, 1.0, 0.0).astype(BF16)) + carry
        carry = rank[blk - 1:blk, :]
        sel_ref[i * blk:(i + 1) * blk, :] = jnp.where((b > thr) | (eq & (rank <= need)), 1.0, 0.0)
    carry = jnp.zeros((1, LANES), F32)
    for i in range(t // blk):
        cs = _dot(tri16, sel_ref[i * blk:(i + 1) * blk, :].astype(BF16)) + carry
        carry = cs[blk - 1:blk, :]
        csum_ref[i * blk:(i + 1) * blk, :] = cs

    lane = lax.broadcasted_iota(jnp.int32, (1, LANES), 1)
    slot = (lax.broadcasted_iota(jnp.int32, (1, cap), 1) + 1).astype(F32)
    tile_rows = min(ROUTE_TILE, t)
    for e in range(N_EXPERTS):
        def tile(i, res, e=e):
            t0 = pl.multiple_of(i * tile_rows, tile_rows)
            a = aff_ref[pl.ds(r0 + t0, tile_rows), :][:, e:e + 1]
            a_hi = a.astype(BF16).astype(F32)
            a_mid = (a - a_hi).astype(BF16).astype(F32)
            a_lo = a - a_hi - a_mid
            tok = t0 + lax.broadcasted_iota(jnp.int32, (tile_rows, 1), 0)
            tok_lo = (tok & 63).astype(F32)
            tok_hi = lax.shift_right_logical(tok, 6).astype(F32)
            feat = jnp.where(lane == 0, tok_lo,
                   jnp.where(lane == 1, tok_hi,
                   jnp.where(lane == 2, a_hi,
                   jnp.where(lane == 3, a_mid,
                   jnp.where(lane == 4, a_lo, 0.0))))).astype(BF16)
            cs = (csum_ref[pl.ds(t0, tile_rows), :] * sel_ref[pl.ds(t0, tile_rows), :])[:, e:e + 1]
            onehot = jnp.where(cs == slot, 1.0, 0.0).astype(BF16)
            return res + _dot_tn(feat, onehot)

        res = lax.fori_loop(0, t // tile_rows, tile, jnp.zeros((LANES, cap), F32))
        idx = res[0:1, :] + 64.0 * res[1:2, :]
        idx_ref[e:e + 1, :] = idx.astype(jnp.int32) + row_base
        gate_ref[e:e + 1, :] = res[2:3, :] + res[3:4, :] + res[4:5, :]


def _route_kernel(aff_ref, il_ref, gl_ref, ic_ref, gc_ref, sel_ref, csum_ref, *, l_ctx, cap_lat, cap_ctx):
    t_all = aff_ref.shape[0]
    base = pl.program_id(0) * t_all
    _select_segment(aff_ref, 0, l_ctx, cap_ctx, base, ic_ref, gc_ref, sel_ref, csum_ref)
    _select_segment(aff_ref, l_ctx, t_all - l_ctx, cap_lat, base + l_ctx, il_ref, gl_ref, sel_ref, csum_ref)


def _route(aff, n_batch, t_all, l_ctx):
    cap_lat = EC_CAPACITY * (t_all - l_ctx) // N_EXPERTS
    cap_ctx = EC_CAPACITY * l_ctx // N_EXPERTS
    spec = lambda cap: pl.BlockSpec((None, N_EXPERTS, cap), lambda b: (b, 0, 0))
    shp = lambda cap, dt: jax.ShapeDtypeStruct((n_batch, N_EXPERTS, cap), dt)
    return pl.pallas_call(
        functools.partial(_route_kernel, l_ctx=l_ctx, cap_lat=cap_lat, cap_ctx=cap_ctx),
        grid=(n_batch,),
        in_specs=[pl.BlockSpec((t_all, LANES), lambda b: (b, 0))],
        out_specs=[spec(cap_lat), spec(cap_lat), spec(cap_ctx), spec(cap_ctx)],
        out_shape=[shp(cap_lat, jnp.int32), shp(cap_lat, F32), shp(cap_ctx, jnp.int32), shp(cap_ctx, F32)],
        scratch_shapes=[pltpu.VMEM((t_all, LANES), F32), pltpu.VMEM((t_all, LANES), F32)],
        compiler_params=_cparams(1), name="route",
    )(aff)


MOE_ROWS = 256
MOE_UNROLL = 8


def _moe_kernel(idx_ref, nxt_ref, gate_ref, gt_ref, wg_ref, wu_ref, wd_ref, h_hbm, x_in_hbm, x_hbm,
                hbuf, xbuf, sem_h, sem_x, sem_o, *, n_steps):
    del x_in_hbm
    rows = hbuf.shape[1]
    i = (pl.program_id(0) * pl.num_programs(1) + pl.program_id(1)) * pl.num_programs(2) + pl.program_id(2)
    slot = i % 2

    def issue_gather(ids_ref, sl):
        def body(p8, carry):
            for u in range(MOE_UNROLL):
                p = p8 * MOE_UNROLL + u
                row = ids_ref[0, p]
                pltpu.make_async_copy(h_hbm.at[pl.ds(row, 1)], hbuf.at[sl, pl.ds(p, 1)], sem_h.at[sl]).start()
                pltpu.make_async_copy(x_hbm.at[pl.ds(row, 1)], xbuf.at[sl, pl.ds(p, 1)], sem_x.at[sl]).start()
            return carry
        lax.fori_loop(0, rows // MOE_UNROLL, body, 0)

    def issue_scatter(sl):
        def body(p8, carry):
            for u in range(MOE_UNROLL):
                p = p8 * MOE_UNROLL + u
                pltpu.make_async_copy(xbuf.at[sl, pl.ds(p, 1)], x_hbm.at[pl.ds(idx_ref[0, p], 1)],
                                      sem_o.at[sl]).start()
            return carry
        lax.fori_loop(0, rows // MOE_UNROLL, body, 0)

    def wait_rows(hbm, buf, sem, sl):
        pltpu.make_async_copy(hbm.at[pl.ds(0, rows)], buf.at[sl], sem.at[sl]).wait()

    @pl.when(i == 0)
    def _():
        issue_gather(idx_ref, slot)

    @pl.when(i > 0)
    def _():
        wait_rows(x_hbm, xbuf, sem_o, 1 - slot)

    @pl.when(i + 1 < n_steps)
    def _():
        issue_gather(nxt_ref, 1 - slot)

    wait_rows(h_hbm, hbuf, sem_h, slot)
    xs = hbuf[slot].astype(BF16)
    act = (_silu(_dot(xs, wg_ref[...])) * _dot(xs, wu_ref[...])).astype(BF16)
    y = _dot(act, wd_ref[...]) * gate_ref[...]
    wait_rows(x_hbm, xbuf, sem_x, slot)
    xbuf[slot] = xbuf[slot] + gt_ref[...] * y
    issue_scatter(slot)

    @pl.when(i == n_steps - 1)
    def _():
        wait_rows(x_hbm, xbuf, sem_o, slot)


def _moe(x, h, idx, gates, modl, mod_row_of_batch, wg, wu, wd):
    r, d = x.shape
    n_batch, n_exp, cap = idx.shape
    assert n_batch >= 2
    rows = min(cap, MOE_ROWS)
    nsub = cap // rows
    n_steps = n_exp * n_batch * nsub
    blk = lambda e, b, s: (b * n_exp + e) * nsub + s

    def nxt(e, b, s):
        j = jnp.minimum((e * n_batch + b) * nsub + s + 1, n_steps - 1)
        return blk(j // (nsub * n_batch), (j // nsub) % n_batch, j % nsub)

    idx3 = idx.reshape(n_batch * n_exp * nsub, 1, rows)
    return pl.pallas_call(
        functools.partial(_moe_kernel, n_steps=n_steps),
        grid=(n_exp, n_batch, nsub),
        in_specs=[pl.BlockSpec((None, 1, rows), lambda e, b, s: (blk(e, b, s), 0, 0), memory_space=pltpu.SMEM),
                  pl.BlockSpec((None, 1, rows), lambda e, b, s: (nxt(e, b, s), 0, 0), memory_space=pltpu.SMEM),
                  pl.BlockSpec((None, rows, 1), lambda e, b, s: (blk(e, b, s), 0, 0)),
                  pl.BlockSpec((None, None, 1, d), lambda e, b, s: (mod_row_of_batch(b), 5, 0, 0)),
                  pl.BlockSpec((None, d, EXPERT_FF), lambda e, b, s: (e, 0, 0)),
                  pl.BlockSpec((None, d, EXPERT_FF), lambda e, b, s: (e, 0, 0)),
                  pl.BlockSpec((None, EXPERT_FF, d), lambda e, b, s: (e, 0, 0)),
                  pl.BlockSpec(memory_space=pl.ANY),
                  pl.BlockSpec(memory_space=pl.ANY)],
        out_specs=pl.BlockSpec(memory_space=pl.ANY),
        out_shape=jax.ShapeDtypeStruct((r, d), F32),
        scratch_shapes=[pltpu.VMEM((2, rows, d), F32), pltpu.VMEM((2, rows, d), F32),
                        pltpu.SemaphoreType.DMA((2,)), pltpu.SemaphoreType.DMA((2,)),
                        pltpu.SemaphoreType.DMA((2,))],
        input_output_aliases={8: 0},
        compiler_params=_cparams(3), name="moe",
    )(idx3, idx3, gates.reshape(n_batch * n_exp * nsub, rows, 1), modl, wg, wu, wd, h, x)


def _final_norm_kernel(x_ref, g_ref, o_ref):
    x = x_ref[...]
    o_ref[...] = x * lax.rsqrt(jnp.mean(x * x, axis=-1, keepdims=True) + EPS) * g_ref[...]


def _final_norm(x, g, n_batch, t_all, l_ctx):
    d = x.shape[1]
    tm = ROW_TILE
    tps = t_all // tm
    lat_tiles = (t_all - l_ctx) // tm
    ctx_tiles = l_ctx // tm
    return pl.pallas_call(
        _final_norm_kernel, grid=(n_batch, lat_tiles),
        in_specs=[pl.BlockSpec((tm, d), lambda b, j: (b * tps + ctx_tiles + j, 0)),
                  pl.BlockSpec((1, d), lambda b, j: (0, 0))],
        out_specs=pl.BlockSpec((tm, d), lambda b, j: (b * lat_tiles + j, 0)),
        out_shape=jax.ShapeDtypeStruct((n_batch * (t_all - l_ctx), d), F32),
        compiler_params=_cparams(2), name="final_norm",
    )(x, g.reshape(1, d))


def _rope_tables(pos, dim):
    inv = ROPE_BASE ** (-jnp.arange(0, dim, 2, dtype=F32) / dim)
    ang = pos.astype(F32)[:, None] * inv[None, :]
    return jnp.cos(ang), jnp.sin(ang)


def _retention_tables(t_all):
    cos, sin = _rope_tables(jnp.arange(t_all), HEAD_DK)
    cos_t = jnp.tile(jnp.concatenate([cos, cos], axis=1), (1, LANES // HEAD_DK))
    sin_t = jnp.tile(jnp.concatenate([-sin, sin], axis=1), (1, LANES // HEAD_DK))
    return cos_t, sin_t


def _axial_tables(s_len):
    rows = s_len // GRID_W
    row_pos = jnp.repeat(jnp.arange(rows), GRID_W)
    col_pos = jnp.tile(jnp.arange(GRID_W), rows)
    rc, rs = _rope_tables(row_pos, HEAD_DK // 2)
    cc, cs = _rope_tables(col_pos, HEAD_DK // 2)
    cos = jnp.concatenate([rc, rc, cc, cc], axis=1)
    sin = jnp.concatenate([-rs, rs, -cs, cs], axis=1)
    return jnp.tile(cos, (1, LANES // HEAD_DK)), jnp.tile(sin, (1, LANES // HEAD_DK))


def _pick_tm(t_all):
    tm = t_all // 4
    assert tm % 16 == 0
    return tm


def kernel(x, c, ctx, c_ctx, w_ada, b_ada, norm_mix, norm_ffn, w_in, gdn_conv, gdn_A_log, gdn_dt_bias, gdn_norm, ret_decay_exp, ret_norm, pool_w, pool_scale, diff_lambda, diff_norm, w_out, w_router, w_gate, w_up, w_down, final_norm):
    n_batch, s_len, d = x.shape
    l_ctx = ctx.shape[1]
    depth = w_ada.shape[0]
    t_all = l_ctx + s_len
    assert d == D_MODEL and l_ctx == ROW_TILE and l_ctx == RET_CHUNK and s_len % DIFF_TQ == 0
    tm = _pick_tm(t_all)

    xs = jnp.concatenate([ctx, x], axis=1).reshape(n_batch * t_all, d)
    c_all = jnp.concatenate([c_ctx[None, :], c, jnp.zeros((8 - 1 - n_batch, d), F32)], axis=0)
    mod = _ada(c_all, w_ada, b_ada).reshape(depth, 8, ADA_CHUNKS, 1, d)
    ret_cos, ret_sin = _retention_tables(t_all)
    ax_cos, ax_sin = _axial_tables(s_len)

    for i in range(depth):
        lam_init = 0.8 - 0.6 * math.exp(-0.3 * i)
        modl = mod[i]
        w_main = jnp.concatenate([w_in[i][:, :IN_AB0], w_in[i][:, IN_AB1:]], axis=1).astype(BF16)
        w_ab = jnp.pad(w_in[i][:, IN_AB0:IN_AB1], ((0, 0), (0, LANES - (IN_AB1 - IN_AB0)))).astype(BF16)

        h = _normmod(xs, norm_mix[i], modl, 0, t_all)
        proj = _matmul(h, w_main, tm, 1024, "in_proj")
        ab = _matmul(h, w_ab, tm, LANES, "ab_proj")
        col = _gdn_gates(ab, gdn_A_log[i], gdn_dt_bias[i])
        r = col.shape[0]
        colh = jnp.pad(col[:, :80].reshape(r, 5, 2, N_HEADS).transpose(3, 0, 1, 2).reshape(N_HEADS, r, 10),
                       ((0, 0), (0, 0), (0, 6)))
        gg = GDN_GROUP * GDN_CHUNK
        rowh = col[:, :16].reshape(r // gg, gg, 2, N_HEADS).transpose(3, 0, 2, 1)
        conv_w = jnp.pad(gdn_conv[i], ((0, 8 - GDN_CONV_K), (0, 0)))

        y_gdn = _gdn(proj, colh, rowh, conv_w, gdn_norm[i], n_batch, t_all, l_ctx)
        y_ret = _retention(proj, ret_cos, ret_sin, ret_decay_exp[i], ret_norm[i], n_batch, t_all, l_ctx)
        y_pool = _pool(proj, pool_w[i], pool_scale[i], n_batch, t_all, l_ctx)
        y_diff = _diff_attention(proj, ax_cos, ax_sin, diff_lambda[i], diff_norm[i], lam_init,
                                 n_batch, t_all, l_ctx)
        y = jnp.concatenate([y_gdn, y_ret, y_pool, y_diff], axis=1)
        xs = _out_proj(y, w_out[i].astype(BF16), xs, modl, 2, t_all, l_ctx, tm, 512)

        h2, aff = _normmod(xs, norm_ffn[i], modl, 3, t_all, w_router=w_router[i])
        idx_lat, gate_lat, idx_ctx, gate_ctx = _route(aff, n_batch, t_all, l_ctx)
        wg = w_gate[i].astype(BF16)
        wu = w_up[i].astype(BF16)
        wd = w_down[i].astype(BF16)
        xs = _moe(xs, h2, idx_lat, gate_lat, modl, lambda b: 1 + b, wg, wu, wd)
        if i < depth - 1:
            xs = _moe(xs, h2, idx_ctx, gate_ctx, modl, lambda b: 0 * b, wg, wu, wd)
    return _final_norm(xs, final_norm, n_batch, t_all, l_ctx).reshape(n_batch, s_len, d)
```

```python
import functools
import math

import jax
import jax.numpy as jnp
from jax import lax
from jax.experimental import pallas as pl
from jax.experimental.pallas import tpu as pltpu

F32 = jnp.float32
BF16 = jnp.bfloat16
HIGHEST = lax.Precision.HIGHEST

D_MODEL = 4096
GROUP_W = 1024
N_HEADS = 8
HEAD_DV = 128
HEAD_DK = 64
ADA_CHUNKS = 6
EPS = 1e-6
ROPE_BASE = 10000.0
GDN_CONV_K = 5
GDN_CHUNK = 64
RET_CHUNK = 256
POOL_WINDOWS = (2, 4, 8, 16)
POOL_GROUP = 256
N_EXPERTS = 16
EXPERT_FF = 256
EC_CAPACITY = 2
GRID_W = 64
IN_AB0, IN_AB1 = 4096, 4128
MAIN_W = 11264
OFF_GQ, OFF_GK, OFF_GV, OFF_GZ = 0, 1024, 2048, 3072
OFF_RQ, OFF_RK, OFF_RV, OFF_RG = 4096, 4608, 5120, 6144
OFF_POOL = 7168
OFF_DQ, OFF_DK, OFF_DV = 8192, 9216, 10240

LANES = 128
ROW_TILE = 256
VMEM_LIMIT_BYTES = 58 * 1024 * 1024


def _cparams(n_axes):
    return pltpu.CompilerParams(dimension_semantics=("arbitrary",) * n_axes,
                                vmem_limit_bytes=VMEM_LIMIT_BYTES)


def _silu(x):
    return x / (1.0 + jnp.exp(-x))


def _sigmoid(x):
    return 1.0 / (1.0 + jnp.exp(-x))


def _dot(a, b):
    return jnp.dot(a, b, preferred_element_type=F32)


def _dot_nt(a, b):
    return lax.dot_general(a, b, (((1,), (1,)), ((), ())), preferred_element_type=F32)


def _dot_tn(a, b):
    return lax.dot_general(a, b, (((0,), (0,)), ((), ())), preferred_element_type=F32)


def _dot_hi(a, b):
    return jnp.dot(a, b, precision=HIGHEST, preferred_element_type=F32)


def _split16(x):
    hi = x.astype(BF16)
    return hi, (x - hi.astype(F32)).astype(BF16)


def _dot3(a, b):
    (ah, al), (bh, bl) = a, b
    return _dot(jnp.concatenate([ah, al, ah], axis=1), jnp.concatenate([bh, bh, bl], axis=0))


def _swap_halves(x, group):
    half = group // 2
    lane = lax.broadcasted_iota(jnp.int32, x.shape, x.ndim - 1)
    return jnp.where((lane % group) < half,
                     pltpu.roll(x, LANES - half, axis=x.ndim - 1),
                     pltpu.roll(x, half, axis=x.ndim - 1))


def _ada_kernel(c_ref, w_ref, b_ref, o_ref):
    s = _silu(c_ref[...]).astype(BF16)
    o_ref[...] = _dot(s, w_ref[...].astype(BF16)) + b_ref[...]


def _ada(c_all, w_ada, b_ada):
    depth, d, n = w_ada.shape
    tn = 1024
    return pl.pallas_call(
        _ada_kernel,
        grid=(depth, n // tn),
        in_specs=[pl.BlockSpec((8, d), lambda l, j: (0, 0)),
                  pl.BlockSpec((None, d, tn), lambda l, j: (l, 0, j)),
                  pl.BlockSpec((None, 1, tn), lambda l, j: (l, 0, j))],
        out_specs=pl.BlockSpec((None, 8, tn), lambda l, j: (l, 0, j)),
        out_shape=jax.ShapeDtypeStruct((depth, 8, n), F32),
        compiler_params=_cparams(2),
        name="ada",
    )(c_all, w_ada, b_ada.reshape(depth, 1, n))


def _mod_row(i, tiles_per_sample):
    return jnp.where(i % tiles_per_sample == 0, 0, 1 + i // tiles_per_sample)


def _normmod_value(x_ref, g_ref, sh_ref, sc_ref):
    x = x_ref[...]
    ms = jnp.mean(x * x, axis=-1, keepdims=True)
    y = x * lax.rsqrt(ms + EPS) * g_ref[...]
    return y * (1.0 + sc_ref[...]) + sh_ref[...]


def _normmod_kernel(x_ref, g_ref, sh_ref, sc_ref, h_ref):
    h_ref[...] = _normmod_value(x_ref, g_ref, sh_ref, sc_ref).astype(h_ref.dtype)


def _normmod_router_kernel(x_ref, g_ref, sh_ref, sc_ref, wr_ref, h_ref, aff_ref):
    h = _normmod_value(x_ref, g_ref, sh_ref, sc_ref)
    h_ref[...] = h.astype(h_ref.dtype)
    logits = _dot(h.astype(BF16), wr_ref[...].astype(BF16))
    lane = lax.broadcasted_iota(jnp.int32, logits.shape, 1)
    logits = jnp.where(lane < N_EXPERTS, logits, -jnp.inf)
    e = jnp.exp(logits - jnp.max(logits, axis=-1, keepdims=True))
    aff_ref[...] = e / jnp.sum(e, axis=-1, keepdims=True)


def _normmod(x, g, modl, shift_chunk, t_all, w_router=None):
    r, d = x.shape
    tm = ROW_TILE
    tps = t_all // tm
    mod_spec = lambda chunk: pl.BlockSpec(
        (None, None, 1, d), lambda i: (_mod_row(i, tps), chunk, 0, 0))
    in_specs = [pl.BlockSpec((tm, d), lambda i: (i, 0)),
                pl.BlockSpec((1, d), lambda i: (0, 0)),
                mod_spec(shift_chunk), mod_spec(shift_chunk + 1)]
    h_spec = pl.BlockSpec((tm, d), lambda i: (i, 0))
    h_shape = jax.ShapeDtypeStruct((r, d), BF16 if w_router is None else F32)
    if w_router is None:
        return pl.pallas_call(
            _normmod_kernel, grid=(r // tm,), in_specs=in_specs, out_specs=h_spec,
            out_shape=h_shape, compiler_params=_cparams(1), name="normmod",
        )(x, g.reshape(1, d), modl, modl)
    wr = jnp.pad(w_router, ((0, 0), (0, LANES - N_EXPERTS)))
    return pl.pallas_call(
        _normmod_router_kernel, grid=(r // tm,),
        in_specs=in_specs + [pl.BlockSpec((d, LANES), lambda i: (0, 0))],
        out_specs=[h_spec, pl.BlockSpec((tm, LANES), lambda i: (i, 0))],
        out_shape=[h_shape, jax.ShapeDtypeStruct((r, LANES), F32)],
        compiler_params=_cparams(1), name="normmod_router",
    )(x, g.reshape(1, d), modl, modl, wr)


def _mm_kernel(x_ref, w_ref, o_ref):
    o_ref[...] = _dot(x_ref[...], w_ref[...]).astype(o_ref.dtype)


def _matmul(x, w, tm, tn, name):
    m, k = x.shape
    n = w.shape[1]
    return pl.pallas_call(
        _mm_kernel, grid=(m // tm, n // tn),
        in_specs=[pl.BlockSpec((tm, k), lambda i, j: (i, 0)),
                  pl.BlockSpec((k, tn), lambda i, j: (0, j))],
        out_specs=pl.BlockSpec((tm, tn), lambda i, j: (i, j)),
        out_shape=jax.ShapeDtypeStruct((m, n), F32),
        compiler_params=_cparams(2), name=name,
    )(x, w)


def _mm_res_kernel(y_ref, w_ref, x_ref, gs_ref, gc_ref, o_ref, *, tm, tiles_per_sample, l_ctx):
    acc = _dot(y_ref[...], w_ref[...])
    i = pl.program_id(0)
    row = (i % tiles_per_sample) * tm + lax.broadcasted_iota(jnp.int32, (tm, 1), 0)
    gate = jnp.where(row < l_ctx, gc_ref[...], gs_ref[...])
    o_ref[...] = x_ref[...] + gate * acc


def _out_proj(y, w, x, modl, gate_chunk, t_all, l_ctx, tm, tn):
    m, k = y.shape
    n = w.shape[1]
    tps = t_all // tm
    kern = functools.partial(_mm_res_kernel, tm=tm, tiles_per_sample=tps, l_ctx=l_ctx)
    return pl.pallas_call(
        kern, grid=(m // tm, n // tn),
        in_specs=[pl.BlockSpec((tm, k), lambda i, j: (i, 0)),
                  pl.BlockSpec((k, tn), lambda i, j: (0, j)),
                  pl.BlockSpec((tm, tn), lambda i, j: (i, j)),
                  pl.BlockSpec((None, None, 1, tn), lambda i, j: (1 + i // tps, gate_chunk, 0, j)),
                  pl.BlockSpec((None, None, 1, tn), lambda i, j: (0, gate_chunk, 0, j))],
        out_specs=pl.BlockSpec((tm, tn), lambda i, j: (i, j)),
        out_shape=jax.ShapeDtypeStruct((m, n), F32),
        compiler_params=_cparams(2), name="out_proj",
    )(y, w, x, modl, modl)


def _gdn_gate_kernel(ab_ref, alog_ref, dtb_ref, o_ref):
    ab = ab_ref[...]
    lane = lax.broadcasted_iota(jnp.int32, (1, LANES), 1)
    z = ab + dtb_ref[...]
    softplus = jnp.maximum(z, 0.0) + jnp.log1p(jnp.exp(-jnp.abs(z)))
    g = -jnp.exp(alog_ref[...]) * softplus
    beta = _sigmoid(ab)
    c = GDN_CHUNK
    ii = lax.broadcasted_iota(jnp.int32, (c, c), 0)
    jj = lax.broadcasted_iota(jnp.int32, (c, c), 1)
    ltri = jnp.where(ii >= jj, 1.0, 0.0).astype(F32)
    for n in range(ab.shape[0] // c):
        gk = g[n * c:(n + 1) * c]
        pre = _dot_hi(ltri, gk)
        tot = pre[c - 1:c, :]
        suf = tot - pre + gk
        gc = jnp.where(lane < N_HEADS, pre, suf)
        e1 = jnp.exp(gc)
        e2 = jnp.exp(tot - gc)
        gt = jnp.broadcast_to(jnp.exp(tot), gc.shape)
        out = jnp.where(lane < 16, gc,
              jnp.where(lane < 32, beta[n * c:(n + 1) * c],
              jnp.where(lane < 48, pltpu.roll(e1, 32, axis=1),
              jnp.where(lane < 64, pltpu.roll(e2, 48, axis=1),
              jnp.where(lane < 80, pltpu.roll(gt, 64, axis=1), 0.0)))))
        o_ref[n * c:(n + 1) * c, :] = out


def _gdn_gates(ab, a_log, dt_bias):
    r = ab.shape[0]
    tm = ROW_TILE
    pad = lambda p: jnp.pad(p.reshape(1, 2 * N_HEADS), ((0, 0), (0, LANES - 2 * N_HEADS)))
    return pl.pallas_call(
        _gdn_gate_kernel, grid=(r // tm,),
        in_specs=[pl.BlockSpec((tm, LANES), lambda i: (i, 0)),
                  pl.BlockSpec((1, LANES), lambda i: (0, 0)),
                  pl.BlockSpec((1, LANES), lambda i: (0, 0))],
        out_specs=pl.BlockSpec((tm, LANES), lambda i: (i, 0)),
        out_shape=jax.ShapeDtypeStruct((r, LANES), F32),
        compiler_params=_cparams(1), name="gdn_gates",
    )(ab, pad(a_log), pad(dt_bias))


CONV_PAD = 8
GDN_GROUP = 4
GDN_GROUPS_PER_STEP = 2


def _gdn_kernel(q_ref, k_ref, v_ref, z_ref, wq_ref, wk_ref, wv_ref, col_ref, row_ref, gn_ref, y_in_ref,
                o_ref, qn, kn, vn, pad, oacc, u_s, wq_s, ak_s, *, l_ctx):
    del y_in_ref
    t_all = q_ref.shape[0]
    c = GDN_CHUNK
    n_chunks = t_all // c
    nc_ctx = l_ctx // c
    rt = ROW_TILE

    def conv_into(src_ref, w_ref, dst, l2, scale):
        w = w_ref[...]
        for (r0, n) in ((0, l_ctx), (l_ctx, t_all - l_ctx)):
            pad[0:CONV_PAD, :] = jnp.zeros((CONV_PAD, LANES), F32)
            pad[CONV_PAD:CONV_PAD + n, :] = src_ref[r0:r0 + n, :]
            pad[CONV_PAD + n:2 * CONV_PAD + n, :] = jnp.zeros((CONV_PAD, LANES), F32)
            for t0 in range(0, n, rt):
                base = CONV_PAD - (GDN_CONV_K - 1) // 2 + t0
                acc = w[0:1, :] * pad[base:base + rt, :]
                for j in range(1, GDN_CONV_K):
                    acc = acc + w[j:j + 1, :] * pad[base + j:base + j + rt, :]
                s = _silu(acc)
                if l2:
                    s = s * lax.rsqrt(jnp.sum(s * s, axis=-1, keepdims=True) + EPS) * scale
                dst[r0 + t0:r0 + t0 + rt, :] = s

    conv_into(q_ref, wq_ref, qn, True, HEAD_DV ** -0.5)
    conv_into(k_ref, wk_ref, kn, True, 1.0)
    conv_into(v_ref, wv_ref, vn, False, 1.0)
    oacc[...] = jnp.zeros(oacc.shape, F32)

    gr = GDN_GROUP * c
    ii = lax.broadcasted_iota(jnp.int32, (gr, gr), 0)
    jj = lax.broadcasted_iota(jnp.int32, (gr, gr), 1)
    same = (ii // c) == (jj // c)
    eye = jnp.where(ii == jj, 1.0, 0.0).astype(F32)

    def prep_body(i, carry):
        chains = []
        for g in range(GDN_GROUPS_PER_STEP):
            gi = jnp.minimum(i * GDN_GROUPS_PER_STEP + g, n_groups - 1)
            r0 = pl.multiple_of(gi * gr, gr)
            q = qn[pl.ds(r0, gr), :]
            k = kn[pl.ds(r0, gr), :]
            v = vn[pl.ds(r0, gr), :]
            col = col_ref[pl.ds(r0, gr), :]
            k16 = k.astype(BF16)
            kq = _dot_nt(jnp.concatenate([k16, q.astype(BF16)], axis=0), k16)
            for d in range(2):
                bc = lambda j, col=col: jnp.broadcast_to(col[:, j:j + 1], (gr, LANES))
                gc, beta, e1, e2 = bc(d), bc(2 + d), bc(4 + d), bc(6 + d)
                wide = lambda x: jnp.concatenate([x, x], axis=1)
                gcr = row_ref[gi, d:d + 1, :]
                incl = same & ((ii >= jj) if d == 0 else (ii <= jj))
                strict = same & ((ii > jj) if d == 0 else (ii < jj))
                dm = jnp.exp(jnp.where(incl, wide(gc) - gcr, -jnp.inf))
                p = -jnp.where(strict, wide(beta) * kq[0:gr] * dm, 0.0)
                rhs = jnp.concatenate([(v * beta).astype(BF16), (k * (beta * e1)).astype(BF16)], axis=1)
                chains.append(dict(d=d, r0=r0, t=eye + p, ps=_split16(p), rhs=rhs,
                                   qd16=(q * e1).astype(BF16), att16=(kq[gr:2 * gr] * dm).astype(BF16),
                                   kdt16=(k * e2).T.astype(BF16)))
        for _ in range(5):
            for ch in chains:
                ch["ps"] = _split16(_dot3(ch["ps"], ch["ps"]))
            for ch in chains:
                ch["t"] = ch["t"] + _dot3(_split16(ch["t"]), ch["ps"])
        for ch in chains:
            ch["uw"] = _dot(ch["t"].astype(BF16), ch["rhs"])
        for ch in chains:
            d, r0, uw = ch["d"], ch["r0"], ch["uw"]
            w16 = uw[:, HEAD_DV:2 * HEAD_DV].astype(BF16)
            u_s[d, pl.ds(r0, gr), :] = uw[:, 0:HEAD_DV]
            for n in range(GDN_GROUP):
                rows = slice(n * c, (n + 1) * c)
                wq_s[d, pl.ds(2 * r0 + 2 * n * c, c), :] = w16[rows]
                wq_s[d, pl.ds(2 * r0 + 2 * n * c + c, c), :] = ch["qd16"][rows]
                ak_s[d, pl.ds(3 * r0 + 3 * n * c, c), :] = ch["att16"][rows, rows]
                ak_s[d, pl.ds(3 * r0 + 3 * n * c + c, 2 * c), :] = ch["kdt16"][:, rows]
        return carry

    n_groups = t_all // gr
    lax.fori_loop(0, pl.cdiv(n_groups, GDN_GROUPS_PER_STEP), prep_body, 0)

    def chunk(d, ci, state):
        r0 = pl.multiple_of(ci * c, c)
        s16 = state.astype(BF16)
        ws = _dot(wq_s[d, pl.ds(2 * r0, 2 * c), :], s16)
        vn16 = (u_s[d, pl.ds(r0, c), :] - ws[0:c]).astype(BF16)
        av = _dot(ak_s[d, pl.ds(3 * r0, 3 * c), :], vn16)
        gt = col_ref[pl.ds(r0, 1), :][:, 8 + d:9 + d]
        oacc[pl.ds(r0, c), :] += ws[c:2 * c] + av[0:c]
        return state * gt + av[c:3 * c]

    def body(i, carry):
        sf, sb = carry
        cb = jnp.where(i < nc_ctx, nc_ctx - 1 - i, n_chunks - 1 - i + nc_ctx)
        sf = chunk(0, i, sf)
        sb = chunk(1, cb, sb)
        return sf, sb

    zero = jnp.zeros((HEAD_DV, HEAD_DV), F32)
    lax.fori_loop(0, n_chunks, body, (zero, zero))

    def post(i, carry):
        r0 = pl.multiple_of(i * rt, rt)
        o = oacc[pl.ds(r0, rt), :]
        y = o * lax.rsqrt(jnp.mean(o * o, axis=-1, keepdims=True) + EPS) * gn_ref[...]
        o_ref[pl.ds(r0, rt), :] = (y * _silu(z_ref[pl.ds(r0, rt), :])).astype(o_ref.dtype)
        return carry

    lax.fori_loop(0, t_all // rt, post, 0)


def _gdn(proj, y, colh, rowh, conv_w, gdn_norm, n_batch, t_all, l_ctx):
    r = proj.shape[0]
    cb = lambda off: (lambda b, h: (b, off // LANES + h))
    wb = lambda off: (lambda b, h: (0, off // LANES + h))
    seq = lambda im: pl.BlockSpec((t_all, LANES), im)
    ngr = t_all // (GDN_GROUP * GDN_CHUNK)
    return pl.pallas_call(
        functools.partial(_gdn_kernel, l_ctx=l_ctx),
        grid=(n_batch, N_HEADS),
        in_specs=[seq(cb(OFF_GQ)), seq(cb(OFF_GK)), seq(cb(OFF_GV)), seq(cb(OFF_GZ)),
                  pl.BlockSpec((8, LANES), wb(0)), pl.BlockSpec((8, LANES), wb(GROUP_W)),
                  pl.BlockSpec((8, LANES), wb(2 * GROUP_W)),
                  pl.BlockSpec((None, t_all, 16), lambda b, h: (h, b, 0)),
                  pl.BlockSpec((None, ngr, 2, GDN_GROUP * GDN_CHUNK), lambda b, h: (h, b, 0, 0)),
                  pl.BlockSpec((1, LANES), lambda b, h: (0, 0)),
                  pl.BlockSpec(memory_space=pl.ANY)],
        out_specs=pl.BlockSpec((t_all, LANES), lambda b, h: (b, h)),
        out_shape=jax.ShapeDtypeStruct(y.shape, y.dtype), input_output_aliases={10: 0},
        scratch_shapes=[pltpu.VMEM((t_all, LANES), F32), pltpu.VMEM((t_all, LANES), F32),
                        pltpu.VMEM((t_all, LANES), F32),
                        pltpu.VMEM((t_all + 2 * CONV_PAD, LANES), F32),
                        pltpu.VMEM((t_all, LANES), F32),
                        pltpu.VMEM((2, t_all, LANES), F32),
                        pltpu.VMEM((2, 2 * t_all, LANES), BF16),
                        pltpu.VMEM((2, 3 * t_all, GDN_CHUNK), BF16)],
        compiler_params=_cparams(2), name="gdn",
    )(proj, proj, proj, proj, conv_w, conv_w, conv_w, colh, rowh, gdn_norm.reshape(1, LANES), y)


def _ret_kernel(q_ref, k_ref, v_ref, g_ref, cos_ref, sin_ref, rde_ref, rn_ref, y_in_ref, o_ref,
                oacc, dmat, qdec, kdec, gtm, *, l_ctx):
    del y_in_ref
    t_all = q_ref.shape[0]
    c = RET_CHUNK
    n_chunks = t_all // c
    pair = pl.program_id(1)
    lane = lax.broadcasted_iota(jnp.int32, (1, LANES), 1)
    lgv = jnp.log1p(-jnp.exp2(-rde_ref[...]))

    ii = lax.broadcasted_iota(jnp.int32, (c, c), 0)
    jj = lax.broadcasted_iota(jnp.int32, (c, c), 1)
    pos = lax.broadcasted_iota(jnp.int32, (c, LANES), 0).astype(F32)
    lane_c = lax.broadcasted_iota(jnp.int32, (c, LANES), 1)
    srow = lax.broadcasted_iota(jnp.int32, (LANES, 2 * HEAD_DV), 0)
    scol = lax.broadcasted_iota(jnp.int32, (LANES, 2 * HEAD_DV), 1)
    for d in range(2):
        lg = [jnp.sum(jnp.where(lane == 2 * pair + a, lgv[d:d + 1, :], 0.0), axis=1, keepdims=True)
              for a in range(2)]
        for a in range(2):
            dist = (ii - jj) if d == 0 else (jj - ii)
            dmat[a * 2 + d] = jnp.exp(jnp.where(dist >= 0, dist.astype(F32) * lg[a], -jnp.inf))
        lg_l = jnp.where(lane_c < HEAD_DK, lg[0], lg[1])
        qpow = (pos + 1.0) if d == 0 else (c - pos)
        kpow = (c - 1.0 - pos) if d == 0 else pos
        qdec[d] = jnp.exp(qpow * lg_l)
        kdec[d] = jnp.exp(kpow * lg_l)
        blk0 = (srow < HEAD_DK) & (scol < HEAD_DV)
        blk1 = (srow >= HEAD_DK) & (scol >= HEAD_DV)
        gtm[d] = jnp.where(blk0, jnp.exp(c * lg[0]), jnp.where(blk1, jnp.exp(c * lg[1]), 0.0))
    blockmask = jnp.where(((srow < HEAD_DK) & (scol < HEAD_DV)) | ((srow >= HEAD_DK) & (scol >= HEAD_DV)),
                          1.0, 0.0).astype(F32)
    oacc[...] = jnp.zeros(oacc.shape, F32)

    nc_ctx = l_ctx // c

    def body(i, carry):
        cb = jnp.where(i < nc_ctx, nc_ctx - 1 - i, n_chunks - 1 - i + nc_ctx)
        chains = []
        for d, ci, state in ((0, i, carry[0]), (1, cb, carry[1])):
            r0 = pl.multiple_of(ci * c, c)
            cos = cos_ref[pl.ds(r0, c), :]
            sin = sin_ref[pl.ds(r0, c), :]
            q = q_ref[pl.ds(r0, c), :]
            k = k_ref[pl.ds(r0, c), :]
            q = q * cos + _swap_halves(q, HEAD_DK) * sin
            k = (k * cos + _swap_halves(k, HEAD_DK) * sin) * HEAD_DK ** -0.5
            chains.append(dict(d=d, r0=r0, state=state, q=q, k=k, k16=k.astype(BF16),
                               v16=v_ref[pl.ds(r0, c), :].astype(BF16)))
        for ch in chains:
            ch["qk"] = [_dot_nt(jnp.where((lane_c < HEAD_DK) == (a == 0), ch["q"], 0.0).astype(BF16), ch["k16"])
                        for a in range(2)]
        for ch in chains:
            d = ch["d"]
            ch["parts"] = [_dot((ch["qk"][a] * dmat[a * 2 + d]).astype(BF16),
                                ch["v16"][:, a * HEAD_DV:(a + 1) * HEAD_DV]) for a in range(2)]
        new_states = []
        for ch in chains:
            d, state = ch["d"], ch["state"]
            o = jnp.concatenate(ch["parts"], axis=1) + _dot((ch["q"] * qdec[d]).astype(BF16), state.astype(BF16))
            new_states.append(state * gtm[d] + blockmask * _dot_tn((ch["k"] * kdec[d]).astype(BF16), ch["v16"]))
            oacc[pl.ds(ch["r0"], c), :] += o
        return tuple(new_states)

    zero = jnp.zeros((LANES, 2 * HEAD_DV), F32)
    lax.fori_loop(0, n_chunks, body, (zero, zero))

    def post(i, carry):
        r0 = pl.multiple_of(i * c, c)
        o = oacc[pl.ds(r0, c), :]
        gate = g_ref[pl.ds(r0, c), :]
        ys = []
        for a in range(2):
            oa = o[:, a * HEAD_DV:(a + 1) * HEAD_DV]
            mu = jnp.mean(oa, axis=-1, keepdims=True)
            var = jnp.mean(jnp.square(oa - mu), axis=-1, keepdims=True)
            ys.append((oa - mu) * lax.rsqrt(var + EPS) * rn_ref[:, a * HEAD_DV:(a + 1) * HEAD_DV])
        y = jnp.concatenate(ys, axis=1) * _silu(gate)
        o_ref[pl.ds(r0, c), :] = y.astype(o_ref.dtype)
        return carry

    lax.fori_loop(0, n_chunks, post, 0)


def _retention(proj, y, cos_t, sin_t, ret_decay_exp, ret_norm, n_batch, t_all, l_ctx):
    r = proj.shape[0]
    c = RET_CHUNK
    rde = jnp.pad(ret_decay_exp, ((0, 6), (0, LANES - N_HEADS)), constant_values=8.0)
    w2 = 2 * HEAD_DV
    return pl.pallas_call(
        functools.partial(_ret_kernel, l_ctx=l_ctx),
        grid=(n_batch, N_HEADS // 2),
        in_specs=[pl.BlockSpec((t_all, LANES), lambda b, p: (b, OFF_RQ // LANES + p)),
                  pl.BlockSpec((t_all, LANES), lambda b, p: (b, OFF_RK // LANES + p)),
                  pl.BlockSpec((t_all, w2), lambda b, p: (b, OFF_RV // w2 + p)),
                  pl.BlockSpec((t_all, w2), lambda b, p: (b, OFF_RG // w2 + p)),
                  pl.BlockSpec((t_all, LANES), lambda b, p: (0, 0)),
                  pl.BlockSpec((t_all, LANES), lambda b, p: (0, 0)),
                  pl.BlockSpec((8, LANES), lambda b, p: (0, 0)),
                  pl.BlockSpec((1, w2), lambda b, p: (0, p)),
                  pl.BlockSpec(memory_space=pl.ANY)],
        out_specs=pl.BlockSpec((t_all, w2), lambda b, p: (b, GROUP_W // w2 + p)),
        out_shape=jax.ShapeDtypeStruct(y.shape, y.dtype), input_output_aliases={8: 0},
        scratch_shapes=[pltpu.VMEM((t_all, w2), F32),
                        pltpu.VMEM((4, c, c), F32),
                        pltpu.VMEM((2, c, LANES), F32), pltpu.VMEM((2, c, LANES), F32),
                        pltpu.VMEM((2, LANES, w2), F32)],
        compiler_params=_cparams(2), name="retention",
    )(proj, proj, proj, proj, cos_t, sin_t, rde, ret_norm.reshape(1, GROUP_W), y)


POOL_PAD = 8


def _pool_kernel(x_ref, w_ref, sc_ref, y_in_ref, o_ref, pad, *, l_ctx):
    del y_in_ref
    t_all = x_ref.shape[0]
    rt = ROW_TILE
    grp = pl.program_id(1)
    w16 = w_ref[...].astype(BF16)
    for gi, win in enumerate(POOL_WINDOWS):
        @pl.when(grp == gi)
        def _(win=win):
            half = win // 2
            for (r0, n) in ((0, l_ctx), (l_ctx, t_all - l_ctx)):
                pad[0:POOL_PAD, :] = jnp.zeros((POOL_PAD, POOL_GROUP), F32)
                pad[POOL_PAD:POOL_PAD + n, :] = x_ref[r0:r0 + n, :]
                pad[POOL_PAD + n:2 * POOL_PAD + n, :] = jnp.zeros((POOL_PAD, POOL_GROUP), F32)

                for t0 in range(0, n, rt):
                    acc = pad[t0 + POOL_PAD - half:t0 + POOL_PAD - half + rt, :]
                    for j in range(1, win):
                        acc = acc + pad[t0 + POOL_PAD - half + j:t0 + POOL_PAD - half + j + rt, :]
                    t = t0 + lax.broadcasted_iota(jnp.int32, (rt, 1), 0)
                    cnt = jnp.minimum(t + half, n) - jnp.maximum(t - half, 0)
                    x = pad[t0 + POOL_PAD:t0 + POOL_PAD + rt, :]
                    dlt = (acc / cnt.astype(F32) - x).astype(BF16)
                    o_ref[r0 + t0:r0 + t0 + rt, :] = (_dot(dlt, w16) * sc_ref[...]).astype(o_ref.dtype)


def _pool(proj, y, pool_w, pool_scale, n_batch, t_all, l_ctx):
    r = proj.shape[0]
    g = POOL_GROUP
    return pl.pallas_call(
        functools.partial(_pool_kernel, l_ctx=l_ctx),
        grid=(n_batch, len(POOL_WINDOWS)),
        in_specs=[pl.BlockSpec((t_all, g), lambda b, i: (b, OFF_POOL // g + i)),
                  pl.BlockSpec((None, g, g), lambda b, i: (i, 0, 0)),
                  pl.BlockSpec((1, g), lambda b, i: (0, i)),
                  pl.BlockSpec(memory_space=pl.ANY)],
        out_specs=pl.BlockSpec((t_all, g), lambda b, i: (b, 2 * GROUP_W // g + i)),
        out_shape=jax.ShapeDtypeStruct(y.shape, y.dtype), input_output_aliases={3: 0},
        scratch_shapes=[pltpu.VMEM((t_all + 2 * POOL_PAD, g), F32)],
        compiler_params=_cparams(2), name="pool",
    )(proj, pool_w, pool_scale.reshape(1, GROUP_W), y)


DIFF_TQ = 512
LOG2E = 1.4426950408889634


def _diff_kernel(q_ref, k_ref, v_ref, cos_ref, sin_ref, dl_ref, dn_ref, y_in_ref, o_ref, k16, v16,
                 *, l_ctx, lam_init):
    del y_in_ref
    t_all = k_ref.shape[0]
    tq = DIFF_TQ
    j = pl.program_id(2)
    half = HEAD_DK // 2

    def rope(x, r0, n):
        return x * cos_ref[pl.ds(r0, n), :] + _swap_halves(x, half) * sin_ref[pl.ds(r0, n), :]

    @pl.when(j == 0)
    def _():
        k16[0:l_ctx, :] = k_ref[0:l_ctx, :].astype(BF16)

        def ktile(i, carry):
            r0 = pl.multiple_of(i * tq, tq)
            k16[pl.ds(l_ctx + r0, tq), :] = rope(k_ref[pl.ds(l_ctx + r0, tq), :], r0, tq).astype(BF16)
            return carry

        lax.fori_loop(0, (t_all - l_ctx) // tq, ktile, 0)
        ones_col = jnp.where(lax.broadcasted_iota(jnp.int32, (t_all, LANES), 1) == 0, 1.0, 0.0)
        v16[...] = jnp.concatenate([v_ref[...], ones_col], axis=1).astype(BF16)

    dl = dl_ref[...]
    lam = (jnp.exp(jnp.sum(dl[0:1, :] * dl[1:2, :], axis=1, keepdims=True))
           - jnp.exp(jnp.sum(dl[2:3, :] * dl[3:4, :], axis=1, keepdims=True)) + lam_init)
    lane = lax.broadcasted_iota(jnp.int32, (1, LANES), 1)

    def attend(q, nk, row0):
        kk = k16[0:nk, :]
        qs = q * (HEAD_DK ** -0.5 * LOG2E)
        hk = (nk // 2) // LANES * LANES
        outs = []
        for m in range(2):
            qm = jnp.where((lane < HEAD_DK) == (m == 0), qs, 0.0).astype(BF16)
            s = _dot_nt(qm, kk)
            e16 = jnp.exp2(s - jnp.max(s, axis=-1, keepdims=True)).astype(BF16)
            pv = _dot(e16[:, 0:hk], v16[0:hk, :]) + _dot(e16[:, hk:nk], v16[hk:nk, :])
            outs.append(pv[:, 0:HEAD_DV] / pv[:, HEAD_DV:HEAD_DV + 1])
        o = outs[0] - lam * outs[1]
        y = o * lax.rsqrt(jnp.mean(o * o, axis=-1, keepdims=True) + EPS) * dn_ref[...]
        o_ref[pl.ds(row0, q.shape[0]), :] = (y * (1.0 - lam_init)).astype(o_ref.dtype)

    @pl.when(j == 0)
    def _():
        attend(q_ref[0:l_ctx, :], l_ctx, 0)

    @pl.when(j > 0)
    def _():
        r0 = pl.multiple_of((j - 1) * tq, tq)
        attend(rope(q_ref[pl.ds(l_ctx + r0, tq), :], r0, tq), t_all, pl.multiple_of(l_ctx + r0, ROW_TILE))


def _diff_attention(proj, y, cos_t, sin_t, diff_lambda, diff_norm, lam_init, n_batch, t_all, l_ctx):
    r = proj.shape[0]
    s_len = t_all - l_ctx
    nj = 1 + s_len // DIFF_TQ
    dl = jnp.pad(diff_lambda, ((0, 4), (0, LANES - HEAD_DK)))
    seq = lambda off: pl.BlockSpec((t_all, LANES), lambda b, h, j: (b, off // LANES + h))
    tab = pl.BlockSpec((s_len, LANES), lambda b, h, j: (0, 0))
    return pl.pallas_call(
        functools.partial(_diff_kernel, l_ctx=l_ctx, lam_init=lam_init),
        grid=(n_batch, N_HEADS, nj),
        in_specs=[seq(OFF_DQ), seq(OFF_DK), seq(OFF_DV), tab, tab,
                  pl.BlockSpec((8, LANES), lambda b, h, j: (0, 0)),
                  pl.BlockSpec((1, LANES), lambda b, h, j: (0, 0)),
                  pl.BlockSpec(memory_space=pl.ANY)],
        out_specs=pl.BlockSpec((t_all, LANES), lambda b, h, j: (b, 3 * GROUP_W // LANES + h)),
        out_shape=jax.ShapeDtypeStruct(y.shape, y.dtype), input_output_aliases={7: 0},
        scratch_shapes=[pltpu.VMEM((t_all, LANES), BF16), pltpu.VMEM((t_all, 2 * LANES), BF16)],
        compiler_params=_cparams(3), name="diff_attn",
    )(proj, proj, proj, cos_t, sin_t, dl, diff_norm.reshape(1, LANES), y)


CUM_BLOCK = 256
ROUTE_TILE = 1024
ONE_BITS_PLUS = 0x3F800001


def _select_segment(aff_ref, r0, t, cap, row_base, idx_ref, gate_ref, sel_ref, csum_ref):
    def bis(_, carry):
        lo, hi = carry
        bits = lax.bitcast_convert_type(aff_ref[r0:r0 + t, :], jnp.int32)
        mid = lo + lax.shift_right_logical(hi - lo, 1)
        cnt = jnp.sum((bits >= mid).astype(jnp.int32), axis=0, keepdims=True)
        ok = cnt >= cap
        return jnp.where(ok, mid, lo), jnp.where(ok, hi, mid)

    lo0 = jnp.zeros((1, LANES), jnp.int32)
    hi0 = jnp.full((1, LANES), ONE_BITS_PLUS, jnp.int32)
    thr, _ = lax.fori_loop(0, 31, bis, (lo0, hi0))
    bits = lax.bitcast_convert_type(aff_ref[r0:r0 + t, :], jnp.int32)
    need = (cap - jnp.sum((bits > thr).astype(jnp.int32), axis=0, keepdims=True)).astype(F32)

    blk = min(CUM_BLOCK, t)
    ii = lax.broadcasted_iota(jnp.int32, (blk, blk), 0)
    jj = lax.broadcasted_iota(jnp.int32, (blk, blk), 1)
    tri16 = jnp.where(ii >= jj, 1.0, 0.0).astype(BF16)
    carry = jnp.zeros((1, LANES), F32)
    for i in range(t // blk):
        b = lax.bitcast_convert_type(aff_ref[r0 + i * blk:r0 + (i + 1) * blk, :], jnp.int32)
        eq = b == thr
        rank = _dot(tri16, jnp.where(eq, 1.0, 0.0).astype(BF16)) + carry
        carry = rank[blk - 1:blk, :]
        sel_ref[i * blk:(i + 1) * blk, :] = jnp.where((b > thr) | (eq & (rank <= need)), 1.0, 0.0)
    carry = jnp.zeros((1, LANES), F32)
    for i in range(t // blk):
        cs = _dot(tri16, sel_ref[i * blk:(i + 1) * blk, :].astype(BF16)) + carry
        carry = cs[blk - 1:blk, :]
        csum_ref[i * blk:(i + 1) * blk, :] = cs

    lane = lax.broadcasted_iota(jnp.int32, (1, LANES), 1)
    slot = (lax.broadcasted_iota(jnp.int32, (1, cap), 1) + 1).astype(F32)
    tile_rows = min(ROUTE_TILE, t)
    for e in range(N_EXPERTS):
        def tile(i, res, e=e):
            t0 = pl.multiple_of(i * tile_rows, tile_rows)
            a = aff_ref[pl.ds(r0 + t0, tile_rows), :][:, e:e + 1]
            a_hi = a.astype(BF16).astype(F32)
            a_mid = (a - a_hi).astype(BF16).astype(F32)
            a_lo = a - a_hi - a_mid
            tok = t0 + lax.broadcasted_iota(jnp.int32, (tile_rows, 1), 0)
            tok_lo = (tok & 63).astype(F32)
            tok_hi = lax.shift_right_logical(tok, 6).astype(F32)
            feat = jnp.where(lane == 0, tok_lo,
                   jnp.where(lane == 1, tok_hi,
                   jnp.where(lane == 2, a_hi,
                   jnp.where(lane == 3, a_mid,
                   jnp.where(lane == 4, a_lo, 0.0))))).astype(BF16)
            cs = (csum_ref[pl.ds(t0, tile_rows), :] * sel_ref[pl.ds(t0, tile_rows), :])[:, e:e + 1]
            onehot = jnp.where(cs == slot, 1.0, 0.0).astype(BF16)
            return res + _dot_tn(feat, onehot)

        res = lax.fori_loop(0, t // tile_rows, tile, jnp.zeros((LANES, cap), F32))
        idx = res[0:1, :] + 64.0 * res[1:2, :]
        idx_ref[e:e + 1, :] = idx.astype(jnp.int32) + row_base
        gate_ref[e:e + 1, :] = res[2:3, :] + res[3:4, :] + res[4:5, :]


def _route_kernel(aff_ref, il_ref, gl_ref, ic_ref, gc_ref, sel_ref, csum_ref, *, l_ctx, cap_lat, cap_ctx):
    t_all = aff_ref.shape[0]
    base = pl.program_id(0) * t_all
    _select_segment(aff_ref, 0, l_ctx, cap_ctx, base, ic_ref, gc_ref, sel_ref, csum_ref)
    _select_segment(aff_ref, l_ctx, t_all - l_ctx, cap_lat, base + l_ctx, il_ref, gl_ref, sel_ref, csum_ref)


def _route(aff, n_batch, t_all, l_ctx):
    cap_lat = EC_CAPACITY * (t_all - l_ctx) // N_EXPERTS
    cap_ctx = EC_CAPACITY * l_ctx // N_EXPERTS
    spec = lambda cap: pl.BlockSpec((None, N_EXPERTS, cap), lambda b: (b, 0, 0))
    shp = lambda cap, dt: jax.ShapeDtypeStruct((n_batch, N_EXPERTS, cap), dt)
    return pl.pallas_call(
        functools.partial(_route_kernel, l_ctx=l_ctx, cap_lat=cap_lat, cap_ctx=cap_ctx),
        grid=(n_batch,),
        in_specs=[pl.BlockSpec((t_all, LANES), lambda b: (b, 0))],
        out_specs=[spec(cap_lat), spec(cap_lat), spec(cap_ctx), spec(cap_ctx)],
        out_shape=[shp(cap_lat, jnp.int32), shp(cap_lat, F32), shp(cap_ctx, jnp.int32), shp(cap_ctx, F32)],
        scratch_shapes=[pltpu.VMEM((t_all, LANES), F32), pltpu.VMEM((t_all, LANES), F32)],
        compiler_params=_cparams(1), name="route",
    )(aff)


MOE_ROWS = 256
MOE_UNROLL = 8


def _moe_kernel(idx_ref, nxt_ref, gate_ref, gt_ref, wg_ref, wu_ref, wd_ref, h_hbm, x_in_hbm, x_hbm,
                hbuf, xbuf, sem_h, sem_x, sem_o, *, n_steps):
    del x_in_hbm
    rows = hbuf.shape[1]
    i = (pl.program_id(0) * pl.num_programs(1) + pl.program_id(1)) * pl.num_programs(2) + pl.program_id(2)
    slot = i % 2

    def issue_gather(ids_ref, sl):
        def body(p8, carry):
            for u in range(MOE_UNROLL):
                p = p8 * MOE_UNROLL + u
                row = ids_ref[0, p]
                pltpu.make_async_copy(h_hbm.at[pl.ds(row, 1)], hbuf.at[sl, pl.ds(p, 1)], sem_h.at[sl]).start()
                pltpu.make_async_copy(x_hbm.at[pl.ds(row, 1)], xbuf.at[sl, pl.ds(p, 1)], sem_x.at[sl]).start()
            return carry
        lax.fori_loop(0, rows // MOE_UNROLL, body, 0)

    def issue_scatter(sl):
        def body(p8, carry):
            for u in range(MOE_UNROLL):
                p = p8 * MOE_UNROLL + u
                pltpu.make_async_copy(xbuf.at[sl, pl.ds(p, 1)], x_hbm.at[pl.ds(idx_ref[0, p], 1)],
                                      sem_o.at[sl]).start()
            return carry
        lax.fori_loop(0, rows // MOE_UNROLL, body, 0)

    def wait_rows(hbm, buf, sem, sl):
        pltpu.make_async_copy(hbm.at[pl.ds(0, rows)], buf.at[sl], sem.at[sl]).wait()

    @pl.when(i == 0)
    def _():
        issue_gather(idx_ref, slot)

    @pl.when(i > 0)
    def _():
        wait_rows(x_hbm, xbuf, sem_o, 1 - slot)

    @pl.when(i + 1 < n_steps)
    def _():
        issue_gather(nxt_ref, 1 - slot)

    wait_rows(h_hbm, hbuf, sem_h, slot)
    xs = hbuf[slot].astype(BF16)
    act = (_silu(_dot(xs, wg_ref[...])) * _dot(xs, wu_ref[...])).astype(BF16)
    y = _dot(act, wd_ref[...]) * gate_ref[...]
    wait_rows(x_hbm, xbuf, sem_x, slot)
    xbuf[slot] = xbuf[slot] + gt_ref[...] * y
    issue_scatter(slot)

    @pl.when(i == n_steps - 1)
    def _():
        wait_rows(x_hbm, xbuf, sem_o, slot)


def _moe(x, h, idx, gates, modl, mod_row_of_batch, wg, wu, wd):
    r, d = x.shape
    n_batch, n_exp, cap = idx.shape
    assert n_batch >= 2
    rows = min(cap, MOE_ROWS)
    nsub = cap // rows
    n_steps = n_exp * n_batch * nsub
    blk = lambda e, b, s: (b * n_exp + e) * nsub + s

    def nxt(e, b, s):
        j = jnp.minimum((e * n_batch + b) * nsub + s + 1, n_steps - 1)
        return blk(j // (nsub * n_batch), (j // nsub) % n_batch, j % nsub)

    idx3 = idx.reshape(n_batch * n_exp * nsub, 1, rows)
    return pl.pallas_call(
        functools.partial(_moe_kernel, n_steps=n_steps),
        grid=(n_exp, n_batch, nsub),
        in_specs=[pl.BlockSpec((None, 1, rows), lambda e, b, s: (blk(e, b, s), 0, 0), memory_space=pltpu.SMEM),
                  pl.BlockSpec((None, 1, rows), lambda e, b, s: (nxt(e, b, s), 0, 0), memory_space=pltpu.SMEM),
                  pl.BlockSpec((None, rows, 1), lambda e, b, s: (blk(e, b, s), 0, 0)),
                  pl.BlockSpec((None, None, 1, d), lambda e, b, s: (mod_row_of_batch(b), 5, 0, 0)),
                  pl.BlockSpec((None, d, EXPERT_FF), lambda e, b, s: (e, 0, 0)),
                  pl.BlockSpec((None, d, EXPERT_FF), lambda e, b, s: (e, 0, 0)),
                  pl.BlockSpec((None, EXPERT_FF, d), lambda e, b, s: (e, 0, 0)),
                  pl.BlockSpec(memory_space=pl.ANY),
                  pl.BlockSpec(memory_space=pl.ANY)],
        out_specs=pl.BlockSpec(memory_space=pl.ANY),
        out_shape=jax.ShapeDtypeStruct((r, d), F32),
        scratch_shapes=[pltpu.VMEM((2, rows, d), F32), pltpu.VMEM((2, rows, d), F32),
                        pltpu.SemaphoreType.DMA((2,)), pltpu.SemaphoreType.DMA((2,)),
                        pltpu.SemaphoreType.DMA((2,))],
        input_output_aliases={8: 0},
        compiler_params=_cparams(3), name="moe",
    )(idx3, idx3, gates.reshape(n_batch * n_exp * nsub, rows, 1), modl, wg, wu, wd, h, x)


def _final_norm_kernel(x_ref, g_ref, o_ref):
    x = x_ref[...]
    o_ref[...] = x * lax.rsqrt(jnp.mean(x * x, axis=-1, keepdims=True) + EPS) * g_ref[...]


def _final_norm(x, g, n_batch, t_all, l_ctx):
    d = x.shape[1]
    tm = ROW_TILE
    tps = t_all // tm
    lat_tiles = (t_all - l_ctx) // tm
    ctx_tiles = l_ctx // tm
    return pl.pallas_call(
        _final_norm_kernel, grid=(n_batch, lat_tiles),
        in_specs=[pl.BlockSpec((tm, d), lambda b, j: (b * tps + ctx_tiles + j, 0)),
                  pl.BlockSpec((1, d), lambda b, j: (0, 0))],
        out_specs=pl.BlockSpec((tm, d), lambda b, j: (b * lat_tiles + j, 0)),
        out_shape=jax.ShapeDtypeStruct((n_batch * (t_all - l_ctx), d), F32),
        compiler_params=_cparams(2), name="final_norm",
    )(x, g.reshape(1, d))


def _rope_tables(pos, dim):
    inv = ROPE_BASE ** (-jnp.arange(0, dim, 2, dtype=F32) / dim)
    ang = pos.astype(F32)[:, None] * inv[None, :]
    return jnp.cos(ang), jnp.sin(ang)


def _retention_tables(t_all):
    cos, sin = _rope_tables(jnp.arange(t_all), HEAD_DK)
    cos_t = jnp.tile(jnp.concatenate([cos, cos], axis=1), (1, LANES // HEAD_DK))
    sin_t = jnp.tile(jnp.concatenate([-sin, sin], axis=1), (1, LANES // HEAD_DK))
    return cos_t, sin_t


def _axial_tables(s_len):
    rows = s_len // GRID_W
    row_pos = jnp.repeat(jnp.arange(rows), GRID_W)
    col_pos = jnp.tile(jnp.arange(GRID_W), rows)
    rc, rs = _rope_tables(row_pos, HEAD_DK // 2)
    cc, cs = _rope_tables(col_pos, HEAD_DK // 2)
    cos = jnp.concatenate([rc, rc, cc, cc], axis=1)
    sin = jnp.concatenate([-rs, rs, -cs, cs], axis=1)
    return jnp.tile(cos, (1, LANES // HEAD_DK)), jnp.tile(sin, (1, LANES // HEAD_DK))


def _pick_tm(t_all):
    tm = t_all // 4
    assert tm % 16 == 0
    return tm


def kernel(x, c, ctx, c_ctx, w_ada, b_ada, norm_mix, norm_ffn, w_in, gdn_conv, gdn_A_log, gdn_dt_bias, gdn_norm, ret_decay_exp, ret_norm, pool_w, pool_scale, diff_lambda, diff_norm, w_out, w_router, w_gate, w_up, w_down, final_norm):
    n_batch, s_len, d = x.shape
    l_ctx = ctx.shape[1]
    depth = w_ada.shape[0]
    t_all = l_ctx + s_len
    assert d == D_MODEL and l_ctx == ROW_TILE and l_ctx == RET_CHUNK and s_len % DIFF_TQ == 0
    tm = _pick_tm(t_all)

    xs = jnp.concatenate([ctx, x], axis=1).reshape(n_batch * t_all, d)
    c_all = jnp.concatenate([c_ctx[None, :], c, jnp.zeros((8 - 1 - n_batch, d), F32)], axis=0)
    mod = _ada(c_all, w_ada, b_ada).reshape(depth, 8, ADA_CHUNKS, 1, d)
    ret_cos, ret_sin = _retention_tables(t_all)
    ax_cos, ax_sin = _axial_tables(s_len)

    for i in range(depth):
        lam_init = 0.8 - 0.6 * math.exp(-0.3 * i)
        modl = mod[i]
        w16 = w_in[i].astype(BF16)
        w_main = jnp.concatenate([w16[:, :IN_AB0], w16[:, IN_AB1:]], axis=1)
        w_ab = jnp.pad(w16[:, IN_AB0:IN_AB1], ((0, 0), (0, LANES - (IN_AB1 - IN_AB0))))

        h = _normmod(xs, norm_mix[i], modl, 0, t_all)
        proj = _matmul(h, w_main, tm, 1024, "in_proj")
        ab = _matmul(h, w_ab, tm, LANES, "ab_proj")
        col = _gdn_gates(ab, gdn_A_log[i], gdn_dt_bias[i])
        r = col.shape[0]
        colh = jnp.pad(col[:, :80].reshape(r, 5, 2, N_HEADS).transpose(3, 0, 1, 2).reshape(N_HEADS, r, 10),
                       ((0, 0), (0, 0), (0, 6)))
        gg = GDN_GROUP * GDN_CHUNK
        rowh = col[:, :16].reshape(r // gg, gg, 2, N_HEADS).transpose(3, 0, 2, 1)
        conv_w = jnp.pad(gdn_conv[i], ((0, 8 - GDN_CONV_K), (0, 0)))

        y = jnp.zeros((xs.shape[0], 4 * GROUP_W), BF16)
        y = _gdn(proj, y, colh, rowh, conv_w, gdn_norm[i], n_batch, t_all, l_ctx)
        y = _retention(proj, y, ret_cos, ret_sin, ret_decay_exp[i], ret_norm[i], n_batch, t_all, l_ctx)
        y = _pool(proj, y, pool_w[i], pool_scale[i], n_batch, t_all, l_ctx)
        y = _diff_attention(proj, y, ax_cos, ax_sin, diff_lambda[i], diff_norm[i], lam_init,
                            n_batch, t_all, l_ctx)
        xs = _out_proj(y, w_out[i].astype(BF16), xs, modl, 2, t_all, l_ctx, tm, 512)

        h2, aff = _normmod(xs, norm_ffn[i], modl, 3, t_all, w_router=w_router[i])
        idx_lat, gate_lat, idx_ctx, gate_ctx = _route(aff, n_batch, t_all, l_ctx)
        wg = w_gate[i].astype(BF16)
        wu = w_up[i].astype(BF16)
        wd = w_down[i].astype(BF16)
        xs = _moe(xs, h2, idx_lat, gate_lat, modl, lambda b: 1 + b, wg, wu, wd)
        if i < depth - 1:
            xs = _moe(xs, h2, idx_ctx, gate_ctx, modl, lambda b: 0 * b, wg, wu, wd)
    return _final_norm(xs, final_norm, n_batch, t_all, l_ctx).reshape(n_batch, s_len, d)
```
